```python
import jax, jax.numpy as jnp
from jax import lax
import numpy as np

D_MODEL = 2048
BATCH = 4
SEQ = 2048
DEPTH = 4

N_MIXERS = 3
N_A = len(range(0, DEPTH, N_MIXERS))
N_B = len(range(1, DEPTH, N_MIXERS))
N_C = len(range(2, DEPTH, N_MIXERS))
EPS = 1e-6
N_MOD = 9
D_FF = 128 * ((8 * D_MODEL // 3 + 127) // 128)
CONV_WIDTH = 31
GLA_HEADS = 4
GLA_DK = D_MODEL // 2
GLA_DV = D_MODEL
GLA_HK = GLA_DK // GLA_HEADS
GLA_HV = GLA_DV // GLA_HEADS
GLA_GATE_RANK = 16
GLA_GATE_TEMP = 16.0
GLA_CHUNK = 64
POOL_WINDOWS = (2, 4, 8, 16)
POOL_GROUP = D_MODEL // len(POOL_WINDOWS)

kernel_name = "hybrid_conv_gla_pool_macaron_encoder"


def rmsnorm(x, g):
    xf = x.astype(jnp.float32)
    y = xf * lax.rsqrt(jnp.mean(xf * xf, axis=-1, keepdims=True) + EPS)
    return (y * g.astype(jnp.float32)).astype(x.dtype)


def layernorm(x, g, b):
    xf = x.astype(jnp.float32)
    mu = jnp.mean(xf, axis=-1, keepdims=True)
    var = jnp.mean(jnp.square(xf - mu), axis=-1, keepdims=True)
    y = (xf - mu) * lax.rsqrt(var + EPS)
    return (y * g.astype(jnp.float32) + b.astype(jnp.float32)).astype(x.dtype)


def modulate(x, g, shift, scale):
    return rmsnorm(x, g) * (1 + scale) + shift


def swiglu(h, w_gate, w_up, w_down):
    return (jax.nn.silu(h @ w_gate) * (h @ w_up)) @ w_down


def conformer_conv(h, w_in, b_in, w_dw, b_dw, ln_g, ln_b, w_out, b_out):
    u = h @ w_in + b_in
    a, gte = jnp.split(u, 2, axis=-1)
    u = a * jax.nn.sigmoid(gte)
    pad = CONV_WIDTH // 2
    u = lax.conv_general_dilated(u, w_dw[:, None, :], (1,), [(pad, pad)],
                                 dimension_numbers=("NWC", "WIO", "NWC"),
                                 feature_group_count=D_MODEL) + b_dw
    u = jax.nn.silu(layernorm(u, ln_g, ln_b))
    return u @ w_out + b_out


def _gla_scan(q, k, v, logg, strict):
    bn, nh, t_len, dk = q.shape
    dv = v.shape[-1]
    n_chunks = t_len // GLA_CHUNK

    def to_chunks(a):
        return jnp.moveaxis(a.reshape(bn, nh, n_chunks, GLA_CHUNK, a.shape[-1]), 2, 0)

    idx = jnp.arange(GLA_CHUNK)
    mask = (idx[None, :] < idx[:, None]) if strict else (idx[None, :] <= idx[:, None])

    def step(state, inp):
        qc, kc, vc, gc = inp
        b = jnp.cumsum(gc, axis=-2)
        inter = jnp.einsum("bhtd,bhde->bhte", qc * jnp.exp(b), state)
        diff = jnp.where(mask[:, :, None], b[..., :, None, :] - b[..., None, :, :], -jnp.inf)
        scores = jnp.einsum("bhtsd,bhsd->bhts", qc[..., :, None, :] * jnp.exp(diff), kc)
        intra = jnp.einsum("bhts,bhse->bhte", scores, vc)
        b_last = b[..., -1:, :]
        state = (jnp.exp(b_last[..., 0, :])[..., None] * state
                 + jnp.einsum("bhsd,bhse->bhde", kc * jnp.exp(b_last - b), vc))
        return state, inter + intra

    s0 = jnp.zeros((bn, nh, dk, dv), jnp.float32)
    _, out = lax.scan(step, s0, (to_chunks(q), to_chunks(k), to_chunks(v), to_chunks(logg)))
    return jnp.moveaxis(out, 0, 2).reshape(bn, nh, t_len, dv)


def gla_mixer(h, w_in, wa1, wa2, ba, norm_g, w_out):
    bn, t_len, _ = h.shape
    proj = h @ w_in
    q, k, v, g = jnp.split(proj, [GLA_DK, 2 * GLA_DK, 2 * GLA_DK + GLA_DV], axis=-1)

    def heads(a, dh):
        return a.reshape(bn, t_len, GLA_HEADS, dh).transpose(0, 2, 1, 3).astype(jnp.float32)

    qh = heads(q, GLA_HK) * (GLA_HK ** -0.5)
    kh = heads(k, GLA_HK)
    vh = heads(v, GLA_HV)

    def log_decay(d):
        z = ((h @ wa1[d]) @ wa2[d] + ba[d]).astype(jnp.float32)
        return heads(jax.nn.log_sigmoid(z) / GLA_GATE_TEMP, GLA_HK)

    flip = lambda a: jnp.flip(a, axis=2)
    o_fwd = _gla_scan(qh, kh, vh, log_decay(0), strict=False)
    o_bwd = flip(_gla_scan(flip(qh), flip(kh), flip(vh), flip(log_decay(1)), strict=True))
    o = (o_fwd + o_bwd).transpose(0, 2, 1, 3)
    o = rmsnorm(o, norm_g).reshape(bn, t_len, GLA_DV).astype(h.dtype)
    return (o * jax.nn.silu(g)) @ w_out


def pool_mixer(h, w_grp, b_grp, scale):
    bn, t_len, _ = h.shape
    hf = h.astype(jnp.float32)
    prefix = jnp.concatenate([jnp.zeros((bn, 1, D_MODEL), jnp.float32), jnp.cumsum(hf, axis=1)], axis=1)
    t = jnp.arange(t_len)
    outs = []
    for gi, w in enumerate(POOL_WINDOWS):
        sl = slice(gi * POOL_GROUP, (gi + 1) * POOL_GROUP)
        lo = jnp.maximum(t - w // 2, 0)
        hi = jnp.minimum(t + (w - w // 2) - 1, t_len - 1)
        pg = prefix[..., sl]
        mean = (jnp.take(pg, hi + 1, axis=1) - jnp.take(pg, lo, axis=1)) / (hi - lo + 1).astype(jnp.float32)[None, :, None]
        d = (mean - hf[..., sl]).astype(h.dtype)
        outs.append(d @ w_grp[gi] + b_grp[gi])
    return jnp.concatenate(outs, axis=-1) * scale


def setup_inputs(seed: int = 0) -> dict:
    key = jax.random.key(seed)
    ks = iter(jax.random.split(key, 40))

    def nrm(shape, std):
        return jax.random.normal(next(ks), shape, jnp.float32) * std

    D, F = D_MODEL, D_FF
    gla_in = 2 * GLA_DK + 2 * GLA_DV
    return {
        "x": nrm((BATCH, SEQ, D), 1.0),
        "c": nrm((BATCH, D), 1.0),
        "ada_w": nrm((DEPTH, D, N_MOD * D), 0.5 * D ** -0.5),
        "ada_b": nrm((DEPTH, N_MOD * D), 0.02),
        "norm_g": 1.0 + nrm((DEPTH, 3, D), 0.02),
        "ffn_w_gate": nrm((DEPTH, 2, D, F), D ** -0.5),
        "ffn_w_up": nrm((DEPTH, 2, D, F), D ** -0.5),
        "ffn_w_down": nrm((DEPTH, 2, F, D), F ** -0.5),
        "conv_w_in": nrm((N_A, D, 2 * D), D ** -0.5),
        "conv_b_in": nrm((N_A, 2 * D), 0.02),
        "conv_w_dw": nrm((N_A, CONV_WIDTH, D), CONV_WIDTH ** -0.5),
        "conv_b_dw": nrm((N_A, D), 0.02),
        "conv_ln_g": 1.0 + nrm((N_A, D), 0.02),
        "conv_ln_b": nrm((N_A, D), 0.02),
        "conv_w_out": nrm((N_A, D, D), D ** -0.5),
        "conv_b_out": nrm((N_A, D), 0.02),
        "gla_w_in": nrm((N_B, D, gla_in), D ** -0.5),
        "gla_wa1": nrm((N_B, 2, D, GLA_GATE_RANK), D ** -0.5),
        "gla_wa2": nrm((N_B, 2, GLA_GATE_RANK, GLA_DK), GLA_GATE_RANK ** -0.5),
        "gla_ba": nrm((N_B, 2, GLA_DK), 0.1),
        "gla_norm_g": 1.0 + nrm((N_B, GLA_HV), 0.02),
        "gla_w_out": nrm((N_B, GLA_DV, D), GLA_DV ** -0.5),
        "pool_w": nrm((N_C, len(POOL_WINDOWS), POOL_GROUP, POOL_GROUP), POOL_GROUP ** -0.5),
        "pool_b": nrm((N_C, len(POOL_WINDOWS), POOL_GROUP), 0.02),
        "pool_scale": 1.0 + nrm((N_C, D), 0.1),
        "final_g": 1.0 + nrm((D,), 0.02),
    }


def reference(x, c, ada_w, ada_b, norm_g, ffn_w_gate, ffn_w_up, ffn_w_down,
              conv_w_in, conv_b_in, conv_w_dw, conv_b_dw, conv_ln_g, conv_ln_b, conv_w_out, conv_b_out,
              gla_w_in, gla_wa1, gla_wa2, gla_ba, gla_norm_g, gla_w_out,
              pool_w, pool_b, pool_scale, final_g):
    c_act = jax.nn.silu(c)
    for i in range(DEPTH):
        mods = (c_act @ ada_w[i] + ada_b[i])[:, None, :]
        sh1, sc1, gt1, sh2, sc2, gt2, sh3, sc3, gt3 = jnp.split(mods, N_MOD, axis=-1)

        h = modulate(x, norm_g[i, 0], sh1, sc1)
        x = x + 0.5 * gt1 * swiglu(h, ffn_w_gate[i, 0], ffn_w_up[i, 0], ffn_w_down[i, 0])

        h = modulate(x, norm_g[i, 1], sh2, sc2)
        kind, j = i % N_MIXERS, i // N_MIXERS
        if kind == 0:
            y = conformer_conv(h, conv_w_in[j], conv_b_in[j], conv_w_dw[j], conv_b_dw[j],
                               conv_ln_g[j], conv_ln_b[j], conv_w_out[j], conv_b_out[j])
        elif kind == 1:
            y = gla_mixer(h, gla_w_in[j], gla_wa1[j], gla_wa2[j], gla_ba[j], gla_norm_g[j], gla_w_out[j])
        else:
            y = pool_mixer(h, pool_w[j], pool_b[j], pool_scale[j])
        x = x + gt2 * y

        h = modulate(x, norm_g[i, 2], sh3, sc3)
        x = x + 0.5 * gt3 * swiglu(h, ffn_w_gate[i, 1], ffn_w_up[i, 1], ffn_w_down[i, 1])
    return rmsnorm(x, final_g)
```

```python
import functools
import math

import numpy as np
import jax
import jax.numpy as jnp
from jax import lax
from jax.experimental import pallas as pl
from jax.experimental.pallas import tpu as pltpu

F32 = jnp.float32
BF16 = jnp.bfloat16

EPS = 1e-6
N_MOD = 9
CONV_WIDTH = 31
GLA_HEADS = 4
GLA_GATE_TEMP = 16.0
POOL_WINDOWS = (2, 4, 8, 16)

LANES = 128
SUBLANES = 8
VMEM_LIMIT = 56 * 1024 * 1024

ROW_CHUNK = 128
GLA_CHUNK = 128
FFN_TM = 512
FFN_TF = 512
MM_TM = 512
CONV_TT = 128
CONV_HALO = 16
POOL_TT = 256
POOL_HALO = 8


def _params(*sem):
    return pltpu.CompilerParams(dimension_semantics=sem, vmem_limit_bytes=VMEM_LIMIT)


def _sigmoid(x):
    return 1.0 / (1.0 + jnp.exp(-x))


def _modnorm(x, ng, sh, sc):
    ms = jnp.mean(x * x, axis=-1, keepdims=True)
    y = x * lax.rsqrt(ms + EPS) * ng
    return y * (1.0 + sc) + sh


def _modnorm_rows(x_ref, h_ref, ng_ref, sh_ref, sc_ref):
    ng, sh, sc = ng_ref[...], sh_ref[...], sc_ref[...]

    def body(r, carry):
        rows = pl.ds(pl.multiple_of(r * ROW_CHUNK, ROW_CHUNK), ROW_CHUNK)
        h_ref[rows, :] = _modnorm(x_ref[rows, :], ng, sh, sc).astype(BF16)
        return carry

    lax.fori_loop(0, x_ref.shape[0] // ROW_CHUNK, body, 0)


def _ada_kernel(c_ref, w_ref, b_ref, o_ref):
    c = c_ref[...]
    ca = (c * _sigmoid(c)).astype(BF16)
    o_ref[0] = jnp.dot(ca, w_ref[0].astype(BF16), preferred_element_type=F32) + b_ref[0]


def _ada_mods(c, ada_w, ada_b):
    depth, d, nd = ada_w.shape
    b = c.shape[0]
    bp = 16
    cp = jnp.zeros((bp, d), F32).at[:b].set(c)
    tn = 1024
    out = pl.pallas_call(
        _ada_kernel,
        grid=(depth, nd // tn),
        in_specs=[
            pl.BlockSpec((bp, d), lambda l, j: (0, 0)),
            pl.BlockSpec((1, d, tn), lambda l, j: (l, 0, j)),
            pl.BlockSpec((1, 1, tn), lambda l, j: (l, 0, j)),
        ],
        out_specs=pl.BlockSpec((1, bp, tn), lambda l, j: (l, 0, j)),
        out_shape=jax.ShapeDtypeStruct((depth, bp, nd), F32),
        compiler_params=_params("parallel", "parallel"),
        name="ada_mods",
    )(cp, ada_w, ada_b.reshape(depth, 1, nd))
    return out[:, :b]


def _ffn_kernel(x_ref, ng_ref, sh_ref, sc_ref, gt_ref, fg_ref, wg_ref, wu_ref, wd_ref,
                o_ref, h_ref, acc_ref, *, final):
    j = pl.program_id(1)

    @pl.when(j == 0)
    def _():
        _modnorm_rows(x_ref, h_ref, ng_ref, sh_ref, sc_ref)

    h = h_ref[...]
    g = jnp.dot(h, wg_ref[...], preferred_element_type=F32)
    u = jnp.dot(h, wu_ref[...], preferred_element_type=F32)
    a = (g * _sigmoid(g) * u).astype(BF16)
    y = jnp.dot(a, wd_ref[...], preferred_element_type=F32)

    @pl.when(j == 0)
    def _():
        acc_ref[...] = y

    @pl.when(j > 0)
    def _():
        acc_ref[...] += y

    @pl.when(j == pl.num_programs(1) - 1)
    def _():
        half_gate = 0.5 * gt_ref[...]
        fg = fg_ref[...]

        def body(r, carry):
            rows = pl.ds(pl.multiple_of(r * ROW_CHUNK, ROW_CHUNK), ROW_CHUNK)
            o = x_ref[rows, :] + half_gate * acc_ref[rows, :]
            if final:
                ms = jnp.mean(o * o, axis=-1, keepdims=True)
                o = o * lax.rsqrt(ms + EPS) * fg
            o_ref[rows, :] = o
            return carry

        lax.fori_loop(0, x_ref.shape[0] // ROW_CHUNK, body, 0)


def _ffn(x, mods, k0, ng, wg, wu, wd, final_g, rows_per_batch, final):
    n, d = x.shape
    fp = wg.shape[1]
    tm, tf = FFN_TM, FFN_TF
    vec = pl.BlockSpec((1, d), lambda i, j: (0, 0))

    def mod(k):
        return pl.BlockSpec((None, None, 1, d), lambda i, j: (i * tm // rows_per_batch, k, 0, 0))

    return pl.pallas_call(
        functools.partial(_ffn_kernel, final=final),
        grid=(n // tm, fp // tf),
        in_specs=[
            pl.BlockSpec((tm, d), lambda i, j: (i, 0)),
            vec, mod(k0), mod(k0 + 1), mod(k0 + 2), vec,
            pl.BlockSpec((d, tf), lambda i, j: (0, j)),
            pl.BlockSpec((d, tf), lambda i, j: (0, j)),
            pl.BlockSpec((tf, d), lambda i, j: (j, 0)),
        ],
        out_specs=pl.BlockSpec((tm, d), lambda i, j: (i, 0)),
        out_shape=jax.ShapeDtypeStruct((n, d), F32),
        scratch_shapes=[pltpu.VMEM((tm, d), BF16), pltpu.VMEM((tm, d), F32)],
        compiler_params=_params("parallel", "arbitrary"),
        name="ffn",
    )(x, ng, mods, mods, mods, final_g, wg, wu, wd)


def _nmm_kernel(*refs, glu, bias, aux):
    x_ref, ng_ref, sh_ref, sc_ref = refs[:4]
    pos = 4
    w_ref = refs[pos]; pos += 1
    w2_ref = b_ref = b2_ref = wa_ref = None
    if glu:
        w2_ref = refs[pos]; pos += 1
    if bias:
        b_ref = refs[pos]; pos += 1
        if glu:
            b2_ref = refs[pos]; pos += 1
    if aux:
        wa_ref = refs[pos]; pos += 1
    o_ref = refs[pos]; pos += 1
    r_ref = None
    if aux:
        r_ref = refs[pos]; pos += 1
    h_ref = refs[pos]

    @pl.when(pl.program_id(1) == 0)
    def _():
        _modnorm_rows(x_ref, h_ref, ng_ref, sh_ref, sc_ref)
        if aux:
            r_ref[...] = jnp.dot(h_ref[...], wa_ref[...], preferred_element_type=F32)

    h = h_ref[...]
    y = jnp.dot(h, w_ref[...], preferred_element_type=F32)
    if bias:
        y = y + b_ref[...]
    if glu:
        y2 = jnp.dot(h, w2_ref[...], preferred_element_type=F32)
        if bias:
            y2 = y2 + b2_ref[...]
        y = y * _sigmoid(y2)
    o_ref[...] = y.astype(o_ref.dtype)


def _norm_mod_matmul(x, mods, k0, ng, w, b, rows_per_batch, *, glu, wa=None, tn=512):
    n, d = x.shape
    nout = w.shape[1] // 2 if glu else w.shape[1]
    tm = MM_TM
    nj = nout // tn
    vec = pl.BlockSpec((1, d), lambda i, j: (0, 0))

    def mod(k):
        return pl.BlockSpec((None, None, 1, d), lambda i, j: (i * tm // rows_per_batch, k, 0, 0))

    in_specs = [pl.BlockSpec((tm, d), lambda i, j: (i, 0)), vec, mod(k0), mod(k0 + 1),
                pl.BlockSpec((d, tn), lambda i, j: (0, j))]
    args = [x, ng, mods, mods, w]
    if glu:
        in_specs.append(pl.BlockSpec((d, tn), lambda i, j: (0, j + nj)))
        args.append(w)
    if b is not None:
        in_specs.append(pl.BlockSpec((1, tn), lambda i, j: (0, j)))
        args.append(b)
        if glu:
            in_specs.append(pl.BlockSpec((1, tn), lambda i, j: (0, j + nj)))
            args.append(b)
    out_specs = pl.BlockSpec((tm, tn), lambda i, j: (i, j))
    out_shape = jax.ShapeDtypeStruct((n, nout), F32)
    if wa is not None:
        in_specs.append(pl.BlockSpec(wa.shape, lambda i, j: (0, 0)))
        args.append(wa)
        out_specs = [out_specs, pl.BlockSpec((tm, wa.shape[1]), lambda i, j: (i, 0))]
        out_shape = [out_shape, jax.ShapeDtypeStruct((n, wa.shape[1]), F32)]
    return pl.pallas_call(
        functools.partial(_nmm_kernel, glu=glu, bias=b is not None, aux=wa is not None),
        grid=(n // tm, nj),
        in_specs=in_specs,
        out_specs=out_specs,
        out_shape=out_shape,
        scratch_shapes=[pltpu.VMEM((tm, d), BF16)],
        compiler_params=_params("parallel", "arbitrary"),
        name="norm_mod_matmul",
    )(*args)


def _proj_res_kernel(*refs, bias):
    a_ref, w_ref = refs[:2]
    pos = 2
    b_ref = None
    if bias:
        b_ref = refs[pos]; pos += 1
    x_ref, gt_ref, o_ref = refs[pos:pos + 3]
    y = jnp.dot(a_ref[...], w_ref[...], preferred_element_type=F32)
    if bias:
        y = y + b_ref[...]
    o_ref[...] = x_ref[...] + gt_ref[...] * y


def _proj_residual(a, w, b, x, mods, kg, rows_per_batch, tn=512):
    n, kdim = a.shape
    d = w.shape[1]
    tm = MM_TM
    in_specs = [pl.BlockSpec((tm, kdim), lambda i, j: (i, 0)),
                pl.BlockSpec((kdim, tn), lambda i, j: (0, j))]
    args = [a, w]
    if b is not None:
        in_specs.append(pl.BlockSpec((1, tn), lambda i, j: (0, j)))
        args.append(b)
    in_specs += [pl.BlockSpec((tm, tn), lambda i, j: (i, j)),
                 pl.BlockSpec((None, None, 1, tn), lambda i, j: (i * tm // rows_per_batch, kg, 0, j))]
    args += [x, mods]
    return pl.pallas_call(
        functools.partial(_proj_res_kernel, bias=b is not None),
        grid=(n // tm, d // tn),
        in_specs=in_specs,
        out_specs=pl.BlockSpec((tm, tn), lambda i, j: (i, j)),
        out_shape=jax.ShapeDtypeStruct((n, d), F32),
        compiler_params=_params("parallel", "arbitrary"),
        name="proj_residual",
    )(*args)


def _dwconv_kernel(prev_ref, main_ref, next_ref, w_ref, bdw_ref, lg_ref, lb_ref, o_ref,
                   buf_ref, conv_ref):
    i = pl.program_id(1)
    tt, d = main_ref.shape
    halo = prev_ref.shape[0]
    pad = CONV_WIDTH // 2
    buf_ref[0:halo, :] = jnp.where(i > 0, prev_ref[...], 0.0)
    buf_ref[halo:halo + tt, :] = main_ref[...]
    buf_ref[halo + tt:, :] = jnp.where(i < pl.num_programs(1) - 1, next_ref[...], 0.0)

    rc, lc = 32, 512
    for r0 in range(0, tt, rc):
        for c0 in range(0, d, lc):
            acc = jnp.zeros((rc, lc), F32)
            for k in range(CONV_WIDTH):
                start = halo - pad + r0 + k
                acc = acc + buf_ref[start:start + rc, c0:c0 + lc] * w_ref[k:k + 1, c0:c0 + lc]
            conv_ref[r0:r0 + rc, c0:c0 + lc] = acc + bdw_ref[:, c0:c0 + lc]

    u = conv_ref[...]
    mu = jnp.mean(u, axis=-1, keepdims=True)
    uc = u - mu
    var = jnp.mean(uc * uc, axis=-1, keepdims=True)
    y = uc * lax.rsqrt(var + EPS) * lg_ref[...] + lb_ref[...]
    o_ref[...] = (y * _sigmoid(y)).astype(o_ref.dtype)


def _dwconv_ln_silu(u, w_dw, b_dw, ln_g, ln_b, batch):
    n, d = u.shape
    t = n // batch
    tt, halo = CONV_TT, CONV_HALO
    nt = t // tt
    hb = tt // halo
    kw = w_dw.shape[0]
    wp = jnp.zeros((32, d), F32).at[:kw].set(w_dw)
    vec = pl.BlockSpec((1, d), lambda b, i: (0, 0))
    return pl.pallas_call(
        _dwconv_kernel,
        grid=(batch, nt),
        in_specs=[
            pl.BlockSpec((halo, d), lambda b, i: (jnp.maximum((b * nt + i) * hb - 1, 0), 0)),
            pl.BlockSpec((tt, d), lambda b, i: (b * nt + i, 0)),
            pl.BlockSpec((halo, d), lambda b, i: (jnp.minimum((b * nt + i + 1) * hb, n // halo - 1), 0)),
            pl.BlockSpec((32, d), lambda b, i: (0, 0)),
            vec, vec, vec,
        ],
        out_specs=pl.BlockSpec((tt, d), lambda b, i: (b * nt + i, 0)),
        out_shape=jax.ShapeDtypeStruct((n, d), BF16),
        scratch_shapes=[pltpu.VMEM((tt + 2 * halo, d), F32), pltpu.VMEM((tt, d), F32)],
        compiler_params=_params("parallel", "arbitrary"),
        name="dwconv_ln_silu",
    )(u, u, u, wp, b_dw, ln_g, ln_b)


def _pool_kernel(prev_ref, x_ref, next_ref, ng_ref, sh_ref, sc_ref, gt_ref, w_ref, b_ref, s_ref,
                 o_ref, hbuf_ref, *, seq_len):
    i = pl.program_id(1)
    tt, d = x_ref.shape
    halo = prev_ref.shape[0]
    ng, sh, sc = ng_ref[...], sh_ref[...], sc_ref[...]
    hbuf_ref[0:halo, :] = jnp.where(i > 0, _modnorm(prev_ref[...], ng, sh, sc), 0.0)
    hbuf_ref[halo:halo + tt, :] = _modnorm(x_ref[...], ng, sh, sc)
    hbuf_ref[halo + tt:, :] = jnp.where(i < pl.num_programs(1) - 1,
                                        _modnorm(next_ref[...], ng, sh, sc), 0.0)
    t = i * tt + lax.broadcasted_iota(jnp.int32, (tt, 1), 0)
    grp = d // len(POOL_WINDOWS)
    for gi, w in enumerate(POOL_WINDOWS):
        cs = slice(gi * grp, (gi + 1) * grp)
        left, right = w // 2, w - w // 2 - 1
        acc = hbuf_ref[halo - left:halo - left + tt, cs]
        for off in range(-left + 1, right + 1):
            acc = acc + hbuf_ref[halo + off:halo + off + tt, cs]
        lo = jnp.maximum(t - left, 0)
        hi = jnp.minimum(t + right, seq_len - 1)
        cnt = (hi - lo + 1).astype(F32)
        dlt = acc / cnt - hbuf_ref[halo:halo + tt, cs]
        y = jnp.dot(dlt.astype(BF16), w_ref[gi], preferred_element_type=F32) + b_ref[:, cs]
        o_ref[:, cs] = x_ref[:, cs] + gt_ref[:, cs] * (y * s_ref[:, cs])


def _pool_sublayer(x, mods, ng, w, b, scale, batch):
    n, d = x.shape
    t = n // batch
    tt, halo = POOL_TT, POOL_HALO
    nt = t // tt
    hb = tt // halo
    vec = pl.BlockSpec((1, d), lambda b_, i: (0, 0))

    def mod(k):
        return pl.BlockSpec((None, None, 1, d), lambda b_, i: (b_, k, 0, 0))

    return pl.pallas_call(
        functools.partial(_pool_kernel, seq_len=t),
        grid=(batch, nt),
        in_specs=[
            pl.BlockSpec((halo, d), lambda b_, i: (jnp.maximum((b_ * nt + i) * hb - 1, 0), 0)),
            pl.BlockSpec((tt, d), lambda b_, i: (b_ * nt + i, 0)),
            pl.BlockSpec((halo, d), lambda b_, i: (jnp.minimum((b_ * nt + i + 1) * hb, n // halo - 1), 0)),
            vec, mod(3), mod(4), mod(5),
            pl.BlockSpec(w.shape, lambda b_, i: (0, 0, 0)),
            vec, vec,
        ],
        out_specs=pl.BlockSpec((tt, d), lambda b_, i: (b_ * nt + i, 0)),
        out_shape=jax.ShapeDtypeStruct((n, d), F32),
        scratch_shapes=[pltpu.VMEM((tt + 2 * halo, d), F32)],
        compiler_params=_params("parallel", "arbitrary"),
        name="pool_sublayer",
    )(x, x, x, ng, mods, mods, mods, w, b, scale)


def _gla_tables(c):
    nl = int(math.log2(c))
    idx = np.arange(c)
    t, r = idx[:, None], idx[None, :]
    stacks, masks = [], []
    for direction in (0, 1):
        fwd = direction == 0
        blocks = [
            (r <= t) if fwd else (r >= t),
            (r > t) if fwd else (r < t),
            np.ones((c, c), bool),
        ]
        msk = [np.eye(c, dtype=bool) if fwd else np.zeros((c, c), bool)]
        for lvl in range(1, nl + 1):
            n = 1 << lvl
            m = (idx // n) * n + n // 2
            mm = m[:, None]
            upper = (idx >= m)[:, None]
            if fwd:
                blk = np.where(upper, (r >= mm) & (r <= t), (r > t) & (r < mm))
                pair = upper & (idx < m)[None, :]
            else:
                blk = np.where(upper, (r >= mm) & (r < t), (r >= t) & (r < mm))
                pair = (~upper) & (idx >= m)[None, :]
            same = (idx // n)[:, None] == (idx // n)[None, :]
            blocks.append(blk)
            msk.append(pair & same)
        stacks.append(np.concatenate(blocks, axis=0))
        masks.append(np.stack(msk))
    return (np.stack(stacks).astype(np.float32), np.stack(masks).astype(np.float32))


def _gla_kernel(q_ref, k_ref, v_ref, og_ref, r_ref, wa2_ref, ba_ref, ng_ref, wst_ref, msk_ref,
                o_ref, oacc_ref, s_ref, *, chunk):
    t_len, hk = q_ref.shape
    hv = v_ref.shape[1]
    c = chunk
    n_chunks = t_len // c
    n_lvl = msk_ref.shape[1] - 1
    qscale = hk ** -0.5
    nt_dims = (((1,), (1,)), ((), ()))

    for direction in (0, 1):
        s_ref[...] = jnp.zeros_like(s_ref)
        wst = wst_ref[direction]
        wa2 = wa2_ref[direction]
        ba = ba_ref[direction]

        def body(it, carry, direction=direction, wst=wst, wa2=wa2, ba=ba):
            ci = it if direction == 0 else n_chunks - 1 - it
            rows = pl.ds(pl.multiple_of(ci * c, c), c)
            z = jnp.dot(r_ref[rows, :].astype(BF16), wa2, preferred_element_type=F32) + ba
            g = (jnp.minimum(z, 0.0) - jnp.log1p(jnp.exp(-jnp.abs(z)))) * (1.0 / GLA_GATE_TEMP)
            g0 = g.astype(BF16)
            r1 = g - g0.astype(F32)
            g1 = r1.astype(BF16)
            g2 = (r1 - g1.astype(F32)).astype(BF16)
            xs = (jnp.dot(wst, g0, preferred_element_type=F32)
                  + jnp.dot(wst, g1, preferred_element_type=F32)
                  + jnp.dot(wst, g2, preferred_element_type=F32))
            f = jnp.exp(xs)

            def blk(b):
                return f[b * c:(b + 1) * c]

            q = q_ref[rows, :] * qscale
            k = k_ref[rows, :]
            vb = v_ref[rows, :].astype(BF16)
            if direction == 0:
                scores = msk_ref[direction, 0] * lax.dot_general(
                    q.astype(BF16), k.astype(BF16), nt_dims, preferred_element_type=F32)
            else:
                scores = jnp.zeros((c, c), F32)
            for lvl in range(1, n_lvl + 1):
                fl = blk(2 + lvl)
                scores = scores + msk_ref[direction, lvl] * lax.dot_general(
                    (q * fl).astype(BF16), (k * fl).astype(BF16), nt_dims,
                    preferred_element_type=F32)
            state = s_ref[...]
            o = (jnp.dot(scores.astype(BF16), vb, preferred_element_type=F32)
                 + jnp.dot((q * blk(0)).astype(BF16), state.astype(BF16),
                           preferred_element_type=F32))
            kt = jnp.transpose(k * blk(1)).astype(BF16)
            decay = jnp.transpose(blk(2))
            decay = jnp.concatenate([decay] * (hv // c), axis=1)
            s_ref[...] = state * decay + jnp.dot(kt, vb, preferred_element_type=F32)
            if direction == 0:
                oacc_ref[rows, :] = o
            else:
                o = o + oacc_ref[rows, :]
                ms = jnp.mean(o * o, axis=-1, keepdims=True)
                y = o * lax.rsqrt(ms + EPS) * ng_ref[...]
                og = og_ref[rows, :]
                o_ref[rows, :] = (y * (og * _sigmoid(og))).astype(o_ref.dtype)
            return carry

        lax.fori_loop(0, n_chunks, body, 0)


def _gla_scan(proj, r, wa2p, ba, norm_g, batch, hk, hv):
    n = proj.shape[0]
    t = n // batch
    c = GLA_CHUNK
    heads = GLA_HEADS
    wst, msk = _gla_tables(c)
    wst = jnp.asarray(wst, BF16)
    msk = jnp.asarray(msk, F32)
    kq, kk = 0, heads
    kv, kg = (2 * heads * hk) // hv, (2 * heads * hk) // hv + heads
    return pl.pallas_call(
        functools.partial(_gla_kernel, chunk=c),
        grid=(batch, heads),
        in_specs=[
            pl.BlockSpec((t, hk), lambda b, h: (b, kq + h)),
            pl.BlockSpec((t, hk), lambda b, h: (b, kk + h)),
            pl.BlockSpec((t, hv), lambda b, h: (b, kv + h)),
            pl.BlockSpec((t, hv), lambda b, h: (b, kg + h)),
            pl.BlockSpec((t, r.shape[1]), lambda b, h: (b, 0)),
            pl.BlockSpec((2, wa2p.shape[1], hk), lambda b, h: (0, 0, h)),
            pl.BlockSpec((2, 1, hk), lambda b, h: (0, 0, h)),
            pl.BlockSpec((1, hv), lambda b, h: (0, 0)),
            pl.BlockSpec(wst.shape, lambda b, h: (0, 0, 0)),
            pl.BlockSpec(msk.shape, lambda b, h: (0, 0, 0, 0)),
        ],
        out_specs=pl.BlockSpec((t, hv), lambda b, h: (b, h)),
        out_shape=jax.ShapeDtypeStruct((n, heads * hv), BF16),
        scratch_shapes=[pltpu.VMEM((t, hv), F32), pltpu.VMEM((hk, hv), F32)],
        compiler_params=_params("parallel", "arbitrary"),
        name="gla_scan",
    )(proj, proj, proj, proj, r, wa2p, ba, norm_g, wst, msk)


def kernel(x, c, ada_w, ada_b, norm_g, ffn_w_gate, ffn_w_up, ffn_w_down, conv_w_in, conv_b_in,
           conv_w_dw, conv_b_dw, conv_ln_g, conv_ln_b, conv_w_out, conv_b_out, gla_w_in, gla_wa1,
           gla_wa2, gla_ba, gla_norm_g, gla_w_out, pool_w, pool_b, pool_scale, final_g):
    batch, t, d = x.shape
    depth = ada_w.shape[0]
    f = ffn_w_gate.shape[-1]
    n = batch * t
    n_mixers = 3

    mods_all = _ada_mods(c, ada_w, ada_b).reshape(depth, batch, N_MOD, 1, d)
    fpad = -f % FFN_TF
    wg_all = jnp.pad(ffn_w_gate.astype(BF16), ((0, 0), (0, 0), (0, 0), (0, fpad)))
    wu_all = jnp.pad(ffn_w_up.astype(BF16), ((0, 0), (0, 0), (0, 0), (0, fpad)))
    wd_all = jnp.pad(ffn_w_down.astype(BF16), ((0, 0), (0, 0), (0, fpad), (0, 0)))
    fg = final_g.reshape(1, d)

    xs = x.reshape(n, d)
    for i in range(depth):
        mods = mods_all[i]
        ng = norm_g[i].reshape(3, 1, d)
        xs = _ffn(xs, mods, 0, ng[0], wg_all[i, 0], wu_all[i, 0], wd_all[i, 0], fg, t, final=False)

        kind, j = i % n_mixers, i // n_mixers
        if kind == 0:
            u = _norm_mod_matmul(xs, mods, 3, ng[1], conv_w_in[j].astype(BF16),
                                 conv_b_in[j].reshape(1, 2 * d), t, glu=True)
            a = _dwconv_ln_silu(u, conv_w_dw[j], conv_b_dw[j].reshape(1, d),
                                conv_ln_g[j].reshape(1, d), conv_ln_b[j].reshape(1, d), batch)
            xs = _proj_residual(a, conv_w_out[j].astype(BF16), conv_b_out[j].reshape(1, d),
                                xs, mods, 5, t)
        elif kind == 1:
            dk = gla_wa2.shape[-1]
            rank = gla_wa1.shape[-1]
            hk, hv = dk // GLA_HEADS, d // GLA_HEADS
            wa1p = jnp.zeros((d, LANES), F32)
            wa2p = jnp.zeros((2, LANES, dk), F32)
            for dr in (0, 1):
                wa1p = wa1p.at[:, dr * rank:(dr + 1) * rank].set(gla_wa1[j, dr])
                wa2p = wa2p.at[dr, dr * rank:(dr + 1) * rank].set(gla_wa2[j, dr])
            proj, r = _norm_mod_matmul(xs, mods, 3, ng[1], gla_w_in[j].astype(BF16), None, t,
                                       glu=False, wa=wa1p.astype(BF16))
            y = _gla_scan(proj, r, wa2p.astype(BF16), gla_ba[j].reshape(2, 1, dk),
                          gla_norm_g[j].reshape(1, hv), batch, hk, hv)
            xs = _proj_residual(y, gla_w_out[j].astype(BF16), None, xs, mods, 5, t)
        else:
            xs = _pool_sublayer(xs, mods, ng[1], pool_w[j].astype(BF16), pool_b[j].reshape(1, d),
                                pool_scale[j].reshape(1, d), batch)

        xs = _ffn(xs, mods, 6, ng[2], wg_all[i, 1], wu_all[i, 1], wd_all[i, 1], fg, t,
                  final=(i == depth - 1))
    return xs.reshape(batch, t, d)
```

```python
import functools
import math

import numpy as np
import jax
import jax.numpy as jnp
from jax import lax
from jax.experimental import pallas as pl
from jax.experimental.pallas import tpu as pltpu

F32 = jnp.float32
BF16 = jnp.bfloat16

EPS = 1e-6
N_MOD = 9
CONV_WIDTH = 31
GLA_HEADS = 4
GLA_GATE_TEMP = 16.0
POOL_WINDOWS = (2, 4, 8, 16)

LANES = 128
SUBLANES = 8
VMEM_LIMIT = 56 * 1024 * 1024

ROW_CHUNK = 128
GLA_CHUNK = 128
FFN_TM = 1024
FFN_TF = 256
FFN_TN = 512
FFN_ROWS = 256
MM_TM = 1024
MM_TN = 1024
CONV_TT = 128
CONV_HALO = 16
CONV_ROWS = 64
POOL_TT = 256
POOL_HALO = 8


def _params(*sem):
    return pltpu.CompilerParams(dimension_semantics=sem, vmem_limit_bytes=VMEM_LIMIT)


def _sigmoid(x):
    return 1.0 / (1.0 + jnp.exp(-x))


def _modnorm(x, ng, sh, sc):
    ms = jnp.mean(x * x, axis=-1, keepdims=True)
    y = x * lax.rsqrt(ms + EPS) * ng
    return y * (1.0 + sc) + sh


def _modnorm_rows(x_ref, h_ref, ng_ref, sh_ref, sc_ref):
    ng, sh, sc = ng_ref[...], sh_ref[...], sc_ref[...]

    def body(r, carry):
        rows = pl.ds(pl.multiple_of(r * ROW_CHUNK, ROW_CHUNK), ROW_CHUNK)
        h_ref[rows, :] = _modnorm(x_ref[rows, :], ng, sh, sc).astype(BF16)
        return carry

    lax.fori_loop(0, x_ref.shape[0] // ROW_CHUNK, body, 0)


def _ada_kernel(c_ref, w_ref, b_ref, o_ref):
    c = c_ref[...]
    ca = (c * _sigmoid(c)).astype(BF16)
    o_ref[0] = jnp.dot(ca, w_ref[0].astype(BF16), preferred_element_type=F32) + b_ref[0]


def _ada_mods(c, ada_w, ada_b):
    depth, d, nd = ada_w.shape
    b = c.shape[0]
    bp = 16
    cp = jnp.zeros((bp, d), F32).at[:b].set(c)
    tn = 1024
    out = pl.pallas_call(
        _ada_kernel,
        grid=(depth, nd // tn),
        in_specs=[
            pl.BlockSpec((bp, d), lambda l, j: (0, 0)),
            pl.BlockSpec((1, d, tn), lambda l, j: (l, 0, j)),
            pl.BlockSpec((1, 1, tn), lambda l, j: (l, 0, j)),
        ],
        out_specs=pl.BlockSpec((1, bp, tn), lambda l, j: (l, 0, j)),
        out_shape=jax.ShapeDtypeStruct((depth, bp, nd), F32),
        compiler_params=_params("parallel", "parallel"),
        name="ada_mods",
    )(cp, ada_w, ada_b.reshape(depth, 1, nd))
    return out[:, :b]


def _ffn_kernel(x_ref, ng_ref, sh_ref, sc_ref, gt_ref, fg_ref, wg_ref, wu_ref, wd_ref,
                wgr_ref, wur_ref, wdr_ref, o_ref, h_ref, *, final, rem):
    j = pl.program_id(1)
    d = x_ref.shape[1]

    def swiglu(g, u):
        return (g * _sigmoid(g) * u).astype(BF16)

    @pl.when(j == 0)
    def _():
        if rem:
            ng, sh, sc = ng_ref[...], sh_ref[...], sc_ref[...]
            wgu = jnp.concatenate([wgr_ref[...].astype(BF16), wur_ref[...].astype(BF16)], axis=1)
            wdr = wdr_ref[...].astype(BF16)
            for r0 in range(0, x_ref.shape[0], FFN_ROWS):
                rows = slice(r0, r0 + FFN_ROWS)
                hb = _modnorm(x_ref[rows, :], ng, sh, sc).astype(BF16)
                h_ref[rows, :] = hb
                gu = jnp.dot(hb, wgu, preferred_element_type=F32)
                a = swiglu(gu[:, :rem], gu[:, rem:])
                o_ref[rows, :] = jnp.dot(a, wdr, preferred_element_type=F32)
        else:
            _modnorm_rows(x_ref, h_ref, ng_ref, sh_ref, sc_ref)
            o_ref[...] = jnp.zeros_like(o_ref)

    h = h_ref[...]
    g = jnp.dot(h, wg_ref[...].astype(BF16), preferred_element_type=F32)
    u = jnp.dot(h, wu_ref[...].astype(BF16), preferred_element_type=F32)
    a = swiglu(g, u)
    for c0 in range(0, d, FFN_TN):
        cols = slice(c0, c0 + FFN_TN)
        o_ref[:, cols] += jnp.dot(a, wd_ref[:, cols].astype(BF16), preferred_element_type=F32)

    @pl.when(j == pl.num_programs(1) - 1)
    def _():
        half_gate = 0.5 * gt_ref[...]
        fg = fg_ref[...]

        def body(r, carry):
            rows = pl.ds(pl.multiple_of(r * ROW_CHUNK, ROW_CHUNK), ROW_CHUNK)
            o = x_ref[rows, :] + half_gate * o_ref[rows, :]
            if final:
                ms = jnp.mean(o * o, axis=-1, keepdims=True)
                o = o * lax.rsqrt(ms + EPS) * fg
            o_ref[rows, :] = o
            return carry

        lax.fori_loop(0, x_ref.shape[0] // ROW_CHUNK, body, 0)


def _ffn(x, mods, k0, ng, wg, wu, wd, final_g, rows_per_batch, final):
    n, d = x.shape
    f = wg.shape[1]
    tm, tf = min(FFN_TM, rows_per_batch), FFN_TF
    nj = f // tf
    rem = f - nj * tf
    rb = rem if rem else tf
    assert f % rb == 0 and rb % LANES == 0
    last = f // rb - 1
    vec = pl.BlockSpec((1, d), lambda i, j: (0, 0))
    once = pl.Buffered(1)

    def mod(k):
        return pl.BlockSpec((None, None, 1, d), lambda i, j: (i * tm // rows_per_batch, k, 0, 0))

    return pl.pallas_call(
        functools.partial(_ffn_kernel, final=final, rem=rem),
        grid=(n // tm, nj),
        in_specs=[
            pl.BlockSpec((tm, d), lambda i, j: (i, 0), pipeline_mode=once),
            vec, mod(k0), mod(k0 + 1), mod(k0 + 2), vec,
            pl.BlockSpec((d, tf), lambda i, j: (0, j)),
            pl.BlockSpec((d, tf), lambda i, j: (0, j)),
            pl.BlockSpec((tf, d), lambda i, j: (j, 0)),
            pl.BlockSpec((d, rb), lambda i, j: (0, last), pipeline_mode=once),
            pl.BlockSpec((d, rb), lambda i, j: (0, last), pipeline_mode=once),
            pl.BlockSpec((rb, d), lambda i, j: (last, 0), pipeline_mode=once),
        ],
        out_specs=pl.BlockSpec((tm, d), lambda i, j: (i, 0)),
        out_shape=jax.ShapeDtypeStruct((n, d), F32),
        scratch_shapes=[pltpu.VMEM((tm, d), BF16)],
        compiler_params=_params("parallel", "arbitrary"),
        name="ffn",
    )(x, ng, mods, mods, mods, final_g, wg, wu, wd, wg, wu, wd)


def _nmm_kernel(*refs, glu, bias, aux):
    x_ref, ng_ref, sh_ref, sc_ref = refs[:4]
    pos = 4
    w_ref = refs[pos]; pos += 1
    w2_ref = b_ref = b2_ref = wa_ref = None
    if glu:
        w2_ref = refs[pos]; pos += 1
    if bias:
        b_ref = refs[pos]; pos += 1
        if glu:
            b2_ref = refs[pos]; pos += 1
    if aux:
        wa_ref = refs[pos]; pos += 1
    o_ref = refs[pos]; pos += 1
    r_ref = None
    if aux:
        r_ref = refs[pos]; pos += 1
    h_ref = refs[pos]

    @pl.when(pl.program_id(1) == 0)
    def _():
        _modnorm_rows(x_ref, h_ref, ng_ref, sh_ref, sc_ref)
        if aux:
            r_ref[...] = jnp.dot(h_ref[...], wa_ref[...], preferred_element_type=F32)

    h = h_ref[...]
    y = jnp.dot(h, w_ref[...], preferred_element_type=F32)
    if bias:
        y = y + b_ref[...]
    if glu:
        y2 = jnp.dot(h, w2_ref[...], preferred_element_type=F32)
        if bias:
            y2 = y2 + b2_ref[...]
        y = y * _sigmoid(y2)
    o_ref[...] = y.astype(o_ref.dtype)


def _norm_mod_matmul(x, mods, k0, ng, w, b, rows_per_batch, *, glu, wa=None):
    n, d = x.shape
    nout = w.shape[1] // 2 if glu else w.shape[1]
    tm, tn = min(MM_TM, rows_per_batch), min(MM_TN, nout)
    nj = nout // tn
    vec = pl.BlockSpec((1, d), lambda i, j: (0, 0))

    def mod(k):
        return pl.BlockSpec((None, None, 1, d), lambda i, j: (i * tm // rows_per_batch, k, 0, 0))

    in_specs = [pl.BlockSpec((tm, d), lambda i, j: (i, 0)), vec, mod(k0), mod(k0 + 1),
                pl.BlockSpec((d, tn), lambda i, j: (0, j))]
    args = [x, ng, mods, mods, w]
    if glu:
        in_specs.append(pl.BlockSpec((d, tn), lambda i, j: (0, j + nj)))
        args.append(w)
    if b is not None:
        in_specs.append(pl.BlockSpec((1, tn), lambda i, j: (0, j)))
        args.append(b)
        if glu:
            in_specs.append(pl.BlockSpec((1, tn), lambda i, j: (0, j + nj)))
            args.append(b)
    out_specs = pl.BlockSpec((tm, tn), lambda i, j: (i, j))
    out_shape = jax.ShapeDtypeStruct((n, nout), F32)
    if wa is not None:
        in_specs.append(pl.BlockSpec(wa.shape, lambda i, j: (0, 0)))
        args.append(wa)
        out_specs = [out_specs, pl.BlockSpec((tm, wa.shape[1]), lambda i, j: (i, 0))]
        out_shape = [out_shape, jax.ShapeDtypeStruct((n, wa.shape[1]), F32)]
    return pl.pallas_call(
        functools.partial(_nmm_kernel, glu=glu, bias=b is not None, aux=wa is not None),
        grid=(n // tm, nj),
        in_specs=in_specs,
        out_specs=out_specs,
        out_shape=out_shape,
        scratch_shapes=[pltpu.VMEM((tm, d), BF16)],
        compiler_params=_params("parallel", "arbitrary"),
        name="norm_mod_matmul",
    )(*args)


def _proj_res_kernel(*refs, bias):
    a_ref, w_ref = refs[:2]
    pos = 2
    b_ref = None
    if bias:
        b_ref = refs[pos]; pos += 1
    x_ref, gt_ref, o_ref = refs[pos:pos + 3]
    y = jnp.dot(a_ref[...], w_ref[...], preferred_element_type=F32)
    if bias:
        y = y + b_ref[...]
    o_ref[...] = x_ref[...] + gt_ref[...] * y


def _proj_residual(a, w, b, x, mods, kg, rows_per_batch):
    n, kdim = a.shape
    d = w.shape[1]
    tm, tn = min(MM_TM, rows_per_batch), min(MM_TN, d)
    in_specs = [pl.BlockSpec((tm, kdim), lambda i, j: (i, 0)),
                pl.BlockSpec((kdim, tn), lambda i, j: (0, j))]
    args = [a, w]
    if b is not None:
        in_specs.append(pl.BlockSpec((1, tn), lambda i, j: (0, j)))
        args.append(b)
    in_specs += [pl.BlockSpec((tm, tn), lambda i, j: (i, j)),
                 pl.BlockSpec((None, None, 1, tn), lambda i, j: (i * tm // rows_per_batch, kg, 0, j))]
    args += [x, mods]
    return pl.pallas_call(
        functools.partial(_proj_res_kernel, bias=b is not None),
        grid=(n // tm, d // tn),
        in_specs=in_specs,
        out_specs=pl.BlockSpec((tm, tn), lambda i, j: (i, j)),
        out_shape=jax.ShapeDtypeStruct((n, d), F32),
        compiler_params=_params("parallel", "arbitrary"),
        name="proj_residual",
    )(*args)


def _dwconv_kernel(prev_ref, main_ref, next_ref, w_ref, bdw_ref, lg_ref, lb_ref, o_ref,
                   buf_ref, conv_ref):
    i = pl.program_id(1)
    tt, d = main_ref.shape
    halo = prev_ref.shape[0]
    pad = CONV_WIDTH // 2
    buf_ref[0:halo, :] = jnp.where(i > 0, prev_ref[...], 0.0)
    buf_ref[halo:halo + tt, :] = main_ref[...]
    buf_ref[halo + tt:, :] = jnp.where(i < pl.num_programs(1) - 1, next_ref[...], 0.0)

    rc, lc = CONV_ROWS, LANES
    win_rows = rc + 2 * halo
    for r0 in range(0, tt, rc):
        for c0 in range(0, d, lc):
            win = buf_ref[r0:r0 + win_rows, c0:c0 + lc]
            acc = jnp.zeros((rc, lc), F32)
            for res in range(SUBLANES):
                shifted = win if res == 0 else pltpu.roll(win, win_rows - res, axis=0)
                for base in range(0, 2 * halo, SUBLANES):
                    k = base + res - (halo - pad)
                    if 0 <= k < CONV_WIDTH:
                        acc = acc + shifted[base:base + rc] * w_ref[k:k + 1, c0:c0 + lc]
            conv_ref[r0:r0 + rc, c0:c0 + lc] = acc + bdw_ref[:, c0:c0 + lc]

    u = conv_ref[...]
    mu = jnp.mean(u, axis=-1, keepdims=True)
    uc = u - mu
    var = jnp.mean(uc * uc, axis=-1, keepdims=True)
    y = uc * lax.rsqrt(var + EPS) * lg_ref[...] + lb_ref[...]
    o_ref[...] = (y * _sigmoid(y)).astype(o_ref.dtype)


def _dwconv_ln_silu(u, w_dw, b_dw, ln_g, ln_b, batch):
    n, d = u.shape
    t = n // batch
    tt, halo = CONV_TT, CONV_HALO
    nt = t // tt
    hb = tt // halo
    kw = w_dw.shape[0]
    wp = jnp.zeros((32, d), F32).at[:kw].set(w_dw)
    vec = pl.BlockSpec((1, d), lambda b, i: (0, 0))
    return pl.pallas_call(
        _dwconv_kernel,
        grid=(batch, nt),
        in_specs=[
            pl.BlockSpec((halo, d), lambda b, i: (jnp.maximum((b * nt + i) * hb - 1, 0), 0)),
            pl.BlockSpec((tt, d), lambda b, i: (b * nt + i, 0)),
            pl.BlockSpec((halo, d), lambda b, i: (jnp.minimum((b * nt + i + 1) * hb, n // halo - 1), 0)),
            pl.BlockSpec((32, d), lambda b, i: (0, 0)),
            vec, vec, vec,
        ],
        out_specs=pl.BlockSpec((tt, d), lambda b, i: (b * nt + i, 0)),
        out_shape=jax.ShapeDtypeStruct((n, d), BF16),
        scratch_shapes=[pltpu.VMEM((tt + 2 * halo, d), F32), pltpu.VMEM((tt, d), F32)],
        compiler_params=_params("parallel", "arbitrary"),
        name="dwconv_ln_silu",
    )(u, u, u, wp, b_dw, ln_g, ln_b)


def _pool_kernel(prev_ref, x_ref, next_ref, ng_ref, sh_ref, sc_ref, gt_ref, w_ref, b_ref, s_ref,
                 o_ref, hbuf_ref, *, seq_len):
    i = pl.program_id(1)
    tt, d = x_ref.shape
    halo = prev_ref.shape[0]
    ng, sh, sc = ng_ref[...], sh_ref[...], sc_ref[...]
    hbuf_ref[0:halo, :] = jnp.where(i > 0, _modnorm(prev_ref[...], ng, sh, sc), 0.0)
    hbuf_ref[halo:halo + tt, :] = _modnorm(x_ref[...], ng, sh, sc)
    hbuf_ref[halo + tt:, :] = jnp.where(i < pl.num_programs(1) - 1,
                                        _modnorm(next_ref[...], ng, sh, sc), 0.0)
    t = i * tt + lax.broadcasted_iota(jnp.int32, (tt, 1), 0)
    grp = d // len(POOL_WINDOWS)
    for gi, w in enumerate(POOL_WINDOWS):
        cs = slice(gi * grp, (gi + 1) * grp)
        left, right = w // 2, w - w // 2 - 1
        acc = hbuf_ref[halo - left:halo - left + tt, cs]
        for off in range(-left + 1, right + 1):
            acc = acc + hbuf_ref[halo + off:halo + off + tt, cs]
        lo = jnp.maximum(t - left, 0)
        hi = jnp.minimum(t + right, seq_len - 1)
        cnt = (hi - lo + 1).astype(F32)
        dlt = acc / cnt - hbuf_ref[halo:halo + tt, cs]
        y = jnp.dot(dlt.astype(BF16), w_ref[gi], preferred_element_type=F32) + b_ref[:, cs]
        o_ref[:, cs] = x_ref[:, cs] + gt_ref[:, cs] * (y * s_ref[:, cs])


def _pool_sublayer(x, mods, ng, w, b, scale, batch):
    n, d = x.shape
    t = n // batch
    tt, halo = POOL_TT, POOL_HALO
    nt = t // tt
    hb = tt // halo
    vec = pl.BlockSpec((1, d), lambda b_, i: (0, 0))

    def mod(k):
        return pl.BlockSpec((None, None, 1, d), lambda b_, i: (b_, k, 0, 0))

    return pl.pallas_call(
        functools.partial(_pool_kernel, seq_len=t),
        grid=(batch, nt),
        in_specs=[
            pl.BlockSpec((halo, d), lambda b_, i: (jnp.maximum((b_ * nt + i) * hb - 1, 0), 0)),
            pl.BlockSpec((tt, d), lambda b_, i: (b_ * nt + i, 0)),
            pl.BlockSpec((halo, d), lambda b_, i: (jnp.minimum((b_ * nt + i + 1) * hb, n // halo - 1), 0)),
            vec, mod(3), mod(4), mod(5),
            pl.BlockSpec(w.shape, lambda b_, i: (0, 0, 0)),
            vec, vec,
        ],
        out_specs=pl.BlockSpec((tt, d), lambda b_, i: (b_ * nt + i, 0)),
        out_shape=jax.ShapeDtypeStruct((n, d), F32),
        scratch_shapes=[pltpu.VMEM((tt + 2 * halo, d), F32)],
        compiler_params=_params("parallel", "arbitrary"),
        name="pool_sublayer",
    )(x, x, x, ng, mods, mods, mods, w, b, scale)


def _gla_tables(c):
    nl = int(math.log2(c))
    idx = np.arange(c)
    t, r = idx[:, None], idx[None, :]
    stacks, masks = [], []
    for direction in (0, 1):
        fwd = direction == 0
        blocks = [
            (r <= t) if fwd else (r >= t),
            (r > t) if fwd else (r < t),
        ]
        msk = [np.eye(c, dtype=bool) if fwd else np.zeros((c, c), bool)]
        for lvl in range(1, nl + 1):
            n = 1 << lvl
            m = (idx // n) * n + n // 2
            mm = m[:, None]
            upper = (idx >= m)[:, None]
            if fwd:
                blk = np.where(upper, (r >= mm) & (r <= t), (r > t) & (r < mm))
                pair = upper & (idx < m)[None, :]
            else:
                blk = np.where(upper, (r >= mm) & (r < t), (r >= t) & (r < mm))
                pair = (~upper) & (idx >= m)[None, :]
            same = (idx // n)[:, None] == (idx // n)[None, :]
            blocks.append(blk)
            msk.append(pair & same)
        blocks.append(np.ones((SUBLANES, c), bool))
        stack = np.concatenate(blocks, axis=0)
        stacks.append(np.concatenate([stack, stack], axis=1))
        masks.append(np.stack(msk))
    return (np.stack(stacks).astype(np.float32), np.stack(masks).astype(np.float32))


def _gla_kernel(q_ref, k_ref, v_ref, og_ref, r_ref, wa2_ref, ba_ref, ng_ref, wst_ref, msk_ref,
                o_ref, oacc_ref, s_ref, vt_ref, qb_ref, kl_ref, dec_ref, *, chunk):
    t_len, hk = q_ref.shape
    c = chunk
    n_chunks = t_len // c
    n_lvl = msk_ref.shape[1] - 1
    qscale = hk ** -0.5
    nt_dims = (((1,), (1,)), ((), ()))

    def chunk_rows(ci):
        return pl.ds(pl.multiple_of(ci * c, c), c)

    def local(ci, carry):
        rows = chunk_rows(ci)
        q = q_ref[rows, :] * qscale
        k = k_ref[rows, :]
        v = v_ref[rows, :]
        vb = v.astype(BF16)
        vt_ref[ci] = jnp.transpose(v).astype(BF16)
        rr = r_ref[rows, :].astype(BF16)
        dirs = (0, 1)
        z = [jnp.dot(rr, wa2_ref[dr], preferred_element_type=F32) + ba_ref[dr] for dr in dirs]
        f = []
        for dr in dirs:
            g = (jnp.minimum(z[dr], 0.0) - jnp.log1p(jnp.exp(-jnp.abs(z[dr])))) * (1.0 / GLA_GATE_TEMP)
            g_hi = g.astype(BF16)
            g_lo = (g - g_hi.astype(F32)).astype(BF16)
            xs = jnp.dot(wst_ref[dr], jnp.concatenate([g_hi, g_lo], axis=0),
                         preferred_element_type=F32)
            f.append(jnp.exp(xs))

        def blk(dr, b):
            return f[dr][b * c:(b + 1) * c]

        scores = [msk_ref[0, 0] * lax.dot_general(q.astype(BF16), k.astype(BF16), nt_dims,
                                                  preferred_element_type=F32),
                  jnp.zeros((c, c), F32)]
        for lvl in range(1, n_lvl + 1):
            for dr in dirs:
                fl = blk(dr, 1 + lvl)
                scores[dr] = scores[dr] + msk_ref[dr, lvl] * lax.dot_general(
                    (q * fl).astype(BF16), (k * fl).astype(BF16), nt_dims,
                    preferred_element_type=F32)
        oacc_ref[rows, :] = jnp.dot((scores[0] + scores[1]).astype(BF16), vb,
                                    preferred_element_type=F32)
        for dr in dirs:
            qb_ref[dr, rows, :] = (q * blk(dr, 0)).astype(BF16)
            kl_ref[dr, rows, :] = (k * blk(dr, 1)).astype(BF16)
            dec_ref[dr, ci] = f[dr][(2 + n_lvl) * c:]
        return carry

    lax.fori_loop(0, n_chunks, local, 0)

    s_ref[...] = jnp.zeros_like(s_ref)

    def carried(it, carry):
        for direction in (0, 1):
            ci = it if direction == 0 else n_chunks - 1 - it
            rows = chunk_rows(ci)
            state = s_ref[direction]
            oacc_ref[rows, :] += lax.dot_general(qb_ref[direction, rows, :], state.astype(BF16),
                                                 nt_dims, preferred_element_type=F32)
            s_ref[direction] = (state * dec_ref[direction, ci][0:1]
                                + jnp.dot(vt_ref[ci], kl_ref[direction, rows, :],
                                          preferred_element_type=F32))
        return carry

    lax.fori_loop(0, n_chunks, carried, 0)

    def finish(ci, carry):
        rows = chunk_rows(ci)
        o = oacc_ref[rows, :]
        ms = jnp.mean(o * o, axis=-1, keepdims=True)
        y = o * lax.rsqrt(ms + EPS) * ng_ref[...]
        og = og_ref[rows, :]
        o_ref[rows, :] = (y * (og * _sigmoid(og))).astype(o_ref.dtype)
        return carry

    lax.fori_loop(0, n_chunks, finish, 0)


def _gla_scan(proj, r, wa2p, ba, norm_g, batch, hk, hv):
    n = proj.shape[0]
    t = n // batch
    c = GLA_CHUNK
    heads = GLA_HEADS
    wst, msk = _gla_tables(c)
    wst = jnp.asarray(wst, BF16)
    msk = jnp.asarray(msk, F32)
    kq, kk = 0, heads
    kv, kg = (2 * heads * hk) // hv, (2 * heads * hk) // hv + heads
    return pl.pallas_call(
        functools.partial(_gla_kernel, chunk=c),
        grid=(batch, heads),
        in_specs=[
            pl.BlockSpec((t, hk), lambda b, h: (b, kq + h)),
            pl.BlockSpec((t, hk), lambda b, h: (b, kk + h)),
            pl.BlockSpec((t, hv), lambda b, h: (b, kv + h)),
            pl.BlockSpec((t, hv), lambda b, h: (b, kg + h)),
            pl.BlockSpec((t, r.shape[1]), lambda b, h: (b, 0)),
            pl.BlockSpec((2, wa2p.shape[1], hk), lambda b, h: (0, 0, h)),
            pl.BlockSpec((2, 1, hk), lambda b, h: (0, 0, h)),
            pl.BlockSpec((1, hv), lambda b, h: (0, 0)),
            pl.BlockSpec(wst.shape, lambda b, h: (0, 0, 0)),
            pl.BlockSpec(msk.shape, lambda b, h: (0, 0, 0, 0)),
        ],
        out_specs=pl.BlockSpec((t, hv), lambda b, h: (b, h)),
        out_shape=jax.ShapeDtypeStruct((n, heads * hv), BF16),
        scratch_shapes=[
            pltpu.VMEM((t, hv), F32),
            pltpu.VMEM((2, hv, hk), F32),
            pltpu.VMEM((t // c, hv, c), BF16),
            pltpu.VMEM((2, t, hk), BF16),
            pltpu.VMEM((2, t, hk), BF16),
            pltpu.VMEM((2, t // c, SUBLANES, hk), F32),
        ],
        compiler_params=_params("parallel", "arbitrary"),
        name="gla_scan",
    )(proj, proj, proj, proj, r, wa2p, ba, norm_g, wst, msk)


def kernel(x, c, ada_w, ada_b, norm_g, ffn_w_gate, ffn_w_up, ffn_w_down, conv_w_in, conv_b_in,
           conv_w_dw, conv_b_dw, conv_ln_g, conv_ln_b, conv_w_out, conv_b_out, gla_w_in, gla_wa1,
           gla_wa2, gla_ba, gla_norm_g, gla_w_out, pool_w, pool_b, pool_scale, final_g):
    batch, t, d = x.shape
    depth = ada_w.shape[0]
    f = ffn_w_gate.shape[-1]
    n = batch * t
    n_mixers = 3

    mods_all = _ada_mods(c, ada_w, ada_b).reshape(depth, batch, N_MOD, 1, d)
    wg_all, wu_all, wd_all = ffn_w_gate, ffn_w_up, ffn_w_down
    fg = final_g.reshape(1, d)

    xs = x.reshape(n, d)
    for i in range(depth):
        mods = mods_all[i]
        ng = norm_g[i].reshape(3, 1, d)
        xs = _ffn(xs, mods, 0, ng[0], wg_all[i, 0], wu_all[i, 0], wd_all[i, 0], fg, t, final=False)

        kind, j = i % n_mixers, i // n_mixers
        if kind == 0:
            u = _norm_mod_matmul(xs, mods, 3, ng[1], conv_w_in[j].astype(BF16),
                                 conv_b_in[j].reshape(1, 2 * d), t, glu=True)
            a = _dwconv_ln_silu(u, conv_w_dw[j], conv_b_dw[j].reshape(1, d),
                                conv_ln_g[j].reshape(1, d), conv_ln_b[j].reshape(1, d), batch)
            xs = _proj_residual(a, conv_w_out[j].astype(BF16), conv_b_out[j].reshape(1, d),
                                xs, mods, 5, t)
        elif kind == 1:
            dk = gla_wa2.shape[-1]
            rank = gla_wa1.shape[-1]
            hk, hv = dk // GLA_HEADS, d // GLA_HEADS
            wa1p = jnp.zeros((d, LANES), F32)
            wa2p = jnp.zeros((2, LANES, dk), F32)
            for dr in (0, 1):
                wa1p = wa1p.at[:, dr * rank:(dr + 1) * rank].set(gla_wa1[j, dr])
                wa2p = wa2p.at[dr, dr * rank:(dr + 1) * rank].set(gla_wa2[j, dr])
            proj, r = _norm_mod_matmul(xs, mods, 3, ng[1], gla_w_in[j].astype(BF16), None, t,
                                       glu=False, wa=wa1p.astype(BF16))
            y = _gla_scan(proj, r, wa2p.astype(BF16), gla_ba[j].reshape(2, 1, dk),
                          gla_norm_g[j].reshape(1, hv), batch, hk, hv)
            xs = _proj_residual(y, gla_w_out[j].astype(BF16), None, xs, mods, 5, t)
        else:
            xs = _pool_sublayer(xs, mods, ng[1], pool_w[j].astype(BF16), pool_b[j].reshape(1, d),
                                pool_scale[j].reshape(1, d), batch)

        xs = _ffn(xs, mods, 6, ng[2], wg_all[i, 1], wu_all[i, 1], wd_all[i, 1], fg, t,
                  final=(i == depth - 1))
    return xs.reshape(batch, t, d)
```

```python
import functools
import math

import numpy as np
import jax
import jax.numpy as jnp
from jax import lax
from jax.experimental import pallas as pl
from jax.experimental.pallas import tpu as pltpu

F32 = jnp.float32
BF16 = jnp.bfloat16

EPS = 1e-6
N_MOD = 9
CONV_WIDTH = 31
GLA_HEADS = 4
GLA_GATE_TEMP = 16.0
POOL_WINDOWS = (2, 4, 8, 16)

LANES = 128
SUBLANES = 8
VMEM_LIMIT = 56 * 1024 * 1024

ROW_CHUNK = 128
GLA_CHUNK = 128
FFN_TM = 1024
FFN_TF = 256
FFN_TN = 512
FFN_ROWS = 256
MM_TM = 1024
MM_TN = 1024
PROJ_TM = 512
PROJ_TN = 512
MM_ROWS = 256
CONV_TT = 128
CONV_HALO = 16
CONV_ROWS = 64
POOL_TT = 256
POOL_HALO = 8


def _params(*sem):
    return pltpu.CompilerParams(dimension_semantics=sem, vmem_limit_bytes=VMEM_LIMIT)


def _sigmoid(x):
    return 1.0 / (1.0 + jnp.exp(-x))


def _modnorm(x, ng, sh, sc):
    ms = jnp.mean(x * x, axis=-1, keepdims=True)
    y = x * lax.rsqrt(ms + EPS) * ng
    return y * (1.0 + sc) + sh


def _modnorm_rows(x_ref, h_ref, ng_ref, sh_ref, sc_ref):
    ng, sh, sc = ng_ref[...], sh_ref[...], sc_ref[...]

    def body(r, carry):
        rows = pl.ds(pl.multiple_of(r * ROW_CHUNK, ROW_CHUNK), ROW_CHUNK)
        h_ref[rows, :] = _modnorm(x_ref[rows, :], ng, sh, sc).astype(BF16)
        return carry

    lax.fori_loop(0, x_ref.shape[0] // ROW_CHUNK, body, 0)


def _ada_kernel(c_ref, w_ref, b_ref, o_ref):
    c = c_ref[...]
    ca = (c * _sigmoid(c)).astype(BF16)
    o_ref[0] = jnp.dot(ca, w_ref[0].astype(BF16), preferred_element_type=F32) + b_ref[0]


def _ada_mods(c, ada_w, ada_b):
    depth, d, nd = ada_w.shape
    b = c.shape[0]
    bp = 16
    cp = jnp.zeros((bp, d), F32).at[:b].set(c)
    tn = 1024
    out = pl.pallas_call(
        _ada_kernel,
        grid=(depth, nd // tn),
        in_specs=[
            pl.BlockSpec((bp, d), lambda l, j: (0, 0)),
            pl.BlockSpec((1, d, tn), lambda l, j: (l, 0, j)),
            pl.BlockSpec((1, 1, tn), lambda l, j: (l, 0, j)),
        ],
        out_specs=pl.BlockSpec((1, bp, tn), lambda l, j: (l, 0, j)),
        out_shape=jax.ShapeDtypeStruct((depth, bp, nd), F32),
        compiler_params=_params("parallel", "parallel"),
        name="ada_mods",
    )(cp, ada_w, ada_b.reshape(depth, 1, nd))
    return out[:, :b]


def _ffn_kernel(x_ref, ng_ref, sh_ref, sc_ref, gt_ref, fg_ref, wg_ref, wu_ref, wd_ref,
                wgr_ref, wur_ref, wdr_ref, o_ref, h_ref, *, final, rem):
    j = pl.program_id(1)
    d = x_ref.shape[1]
    half_gate = 0.5 * gt_ref[...]

    def swiglu(g, u):
        return (g * _sigmoid(g) * u).astype(BF16)

    @pl.when(j == 0)
    def _():
        if rem:
            ng, sh, sc = ng_ref[...], sh_ref[...], sc_ref[...]
            wgu = jnp.concatenate([wgr_ref[...].astype(BF16), wur_ref[...].astype(BF16)], axis=1)
            wdr = wdr_ref[...].astype(BF16)
            for r0 in range(0, x_ref.shape[0], FFN_ROWS):
                rows = slice(r0, r0 + FFN_ROWS)
                hb = _modnorm(x_ref[rows, :], ng, sh, sc).astype(BF16)
                h_ref[rows, :] = hb
                gu = jnp.dot(hb, wgu, preferred_element_type=F32)
                a = swiglu(gu[:, :rem], gu[:, rem:])
                o_ref[rows, :] = x_ref[rows, :] + half_gate * jnp.dot(
                    a, wdr, preferred_element_type=F32)
        else:
            _modnorm_rows(x_ref, h_ref, ng_ref, sh_ref, sc_ref)
            o_ref[...] = x_ref[...]

    h = h_ref[...]
    g = jnp.dot(h, wg_ref[...].astype(BF16), preferred_element_type=F32)
    u = jnp.dot(h, wu_ref[...].astype(BF16), preferred_element_type=F32)
    a = swiglu(g, u)
    for c0 in range(0, d, FFN_TN):
        cols = slice(c0, c0 + FFN_TN)
        o_ref[:, cols] += half_gate[:, cols] * jnp.dot(a, wd_ref[:, cols].astype(BF16),
                                                       preferred_element_type=F32)

    if final:
        @pl.when(j == pl.num_programs(1) - 1)
        def _():
            fg = fg_ref[...]

            def body(r, carry):
                rows = pl.ds(pl.multiple_of(r * ROW_CHUNK, ROW_CHUNK), ROW_CHUNK)
                o = o_ref[rows, :]
                ms = jnp.mean(o * o, axis=-1, keepdims=True)
                o_ref[rows, :] = o * lax.rsqrt(ms + EPS) * fg
                return carry

            lax.fori_loop(0, o_ref.shape[0] // ROW_CHUNK, body, 0)


def _ffn(x, mods, k0, ng, wg, wu, wd, layer, slot, final_g, rows_per_batch, final):
    n, d = x.shape
    f = wg.shape[-1]
    tm, tf = min(FFN_TM, rows_per_batch), FFN_TF
    nj = f // tf
    rem = f - nj * tf
    rb = rem if rem else tf
    assert f % rb == 0 and rb % LANES == 0
    last = f // rb - 1
    nt = n // tm
    vec = pl.BlockSpec((1, d), lambda i, j: (0, 0))
    once = pl.Buffered(1)

    def mod(k):
        return pl.BlockSpec((None, None, 1, d), lambda i, j: (i * tm // rows_per_batch, k, 0, 0))

    return pl.pallas_call(
        functools.partial(_ffn_kernel, final=final, rem=rem),
        grid=(n // tm, nj),
        in_specs=[
            pl.BlockSpec((tm, d), lambda i, j: (jnp.minimum(i + jnp.minimum(j, 1), nt - 1), 0)),
            vec, mod(k0), mod(k0 + 1), mod(k0 + 2), vec,
            pl.BlockSpec((None, None, d, tf), lambda i, j: (layer, slot, 0, j)),
            pl.BlockSpec((None, None, d, tf), lambda i, j: (layer, slot, 0, j)),
            pl.BlockSpec((None, None, tf, d), lambda i, j: (layer, slot, j, 0)),
            pl.BlockSpec((None, None, d, rb), lambda i, j: (layer, slot, 0, last), pipeline_mode=once),
            pl.BlockSpec((None, None, d, rb), lambda i, j: (layer, slot, 0, last), pipeline_mode=once),
            pl.BlockSpec((None, None, rb, d), lambda i, j: (layer, slot, last, 0), pipeline_mode=once),
        ],
        out_specs=pl.BlockSpec((tm, d), lambda i, j: (i, 0)),
        out_shape=jax.ShapeDtypeStruct((n, d), F32),
        scratch_shapes=[pltpu.VMEM((tm, d), BF16)],
        compiler_params=_params("parallel", "arbitrary"),
        name="ffn",
    )(x, ng, mods, mods, mods, final_g, wg, wu, wd, wg, wu, wd)


def _nmm_kernel(*refs, glu, bias, aux):
    x_ref, ng_ref, sh_ref, sc_ref = refs[:4]
    pos = 4
    w_ref = refs[pos]; pos += 1
    w2_ref = b_ref = b2_ref = wa_ref = None
    if glu:
        w2_ref = refs[pos]; pos += 1
    if bias:
        b_ref = refs[pos]; pos += 1
        if glu:
            b2_ref = refs[pos]; pos += 1
    if aux:
        wa_ref = refs[pos]; pos += 1
    o_ref = refs[pos]; pos += 1
    r_ref = None
    if aux:
        r_ref = refs[pos]; pos += 1
    h_ref = refs[pos]

    def project(h):
        y = jnp.dot(h, w_ref[...], preferred_element_type=F32)
        if bias:
            y = y + b_ref[...]
        if glu:
            y2 = jnp.dot(h, w2_ref[...], preferred_element_type=F32)
            if bias:
                y2 = y2 + b2_ref[...]
            y = y * _sigmoid(y2)
        return y.astype(o_ref.dtype)

    @pl.when(pl.program_id(1) == 0)
    def _():
        ng, sh, sc = ng_ref[...], sh_ref[...], sc_ref[...]
        for r0 in range(0, x_ref.shape[0], MM_ROWS):
            rows = slice(r0, r0 + MM_ROWS)
            hb = _modnorm(x_ref[rows, :], ng, sh, sc).astype(BF16)
            h_ref[rows, :] = hb
            if aux:
                r_ref[rows, :] = jnp.dot(hb, wa_ref[...], preferred_element_type=F32)
            o_ref[rows, :] = project(hb)

    @pl.when(pl.program_id(1) > 0)
    def _():
        o_ref[...] = project(h_ref[...])


def _norm_mod_matmul(x, mods, k0, ng, w, b, rows_per_batch, *, glu, wa=None):
    n, d = x.shape
    nout = w.shape[1] // 2 if glu else w.shape[1]
    tm, tn = min(MM_TM, rows_per_batch), min(MM_TN, nout)
    nj = nout // tn
    vec = pl.BlockSpec((1, d), lambda i, j: (0, 0))

    def mod(k):
        return pl.BlockSpec((None, None, 1, d), lambda i, j: (i * tm // rows_per_batch, k, 0, 0))

    in_specs = [pl.BlockSpec((tm, d), lambda i, j: (i, 0)), vec, mod(k0), mod(k0 + 1),
                pl.BlockSpec((d, tn), lambda i, j: (0, j))]
    args = [x, ng, mods, mods, w]
    if glu:
        in_specs.append(pl.BlockSpec((d, tn), lambda i, j: (0, j + nj)))
        args.append(w)
    if b is not None:
        in_specs.append(pl.BlockSpec((1, tn), lambda i, j: (0, j)))
        args.append(b)
        if glu:
            in_specs.append(pl.BlockSpec((1, tn), lambda i, j: (0, j + nj)))
            args.append(b)
    out_specs = pl.BlockSpec((tm, tn), lambda i, j: (i, j))
    out_shape = jax.ShapeDtypeStruct((n, nout), F32)
    if wa is not None:
        in_specs.append(pl.BlockSpec(wa.shape, lambda i, j: (0, 0)))
        args.append(wa)
        out_specs = [out_specs, pl.BlockSpec((tm, wa.shape[1]), lambda i, j: (i, 0))]
        out_shape = [out_shape, jax.ShapeDtypeStruct((n, wa.shape[1]), F32)]
    return pl.pallas_call(
        functools.partial(_nmm_kernel, glu=glu, bias=b is not None, aux=wa is not None),
        grid=(n // tm, nj),
        in_specs=in_specs,
        out_specs=out_specs,
        out_shape=out_shape,
        scratch_shapes=[pltpu.VMEM((tm, d), BF16)],
        compiler_params=_params("parallel", "arbitrary"),
        name="norm_mod_matmul",
    )(*args)


def _proj_res_kernel(*refs, bias):
    a_ref, w_ref = refs[:2]
    pos = 2
    b_ref = None
    if bias:
        b_ref = refs[pos]; pos += 1
    x_ref, gt_ref, o_ref = refs[pos:pos + 3]
    a = a_ref[...]
    tn = min(PROJ_TN, o_ref.shape[1])
    for c0 in range(0, o_ref.shape[1], tn):
        cols = slice(c0, c0 + tn)
        y = jnp.dot(a, w_ref[:, cols], preferred_element_type=F32)
        if bias:
            y = y + b_ref[:, cols]
        o_ref[:, cols] = x_ref[:, cols] + gt_ref[:, cols] * y


def _proj_residual(a, w, b, x, mods, kg, rows_per_batch):
    n, kdim = a.shape
    d = w.shape[1]
    tm = min(PROJ_TM, rows_per_batch)
    in_specs = [pl.BlockSpec((tm, kdim), lambda i: (i, 0)),
                pl.BlockSpec((kdim, d), lambda i: (0, 0), pipeline_mode=pl.Buffered(1))]
    args = [a, w]
    if b is not None:
        in_specs.append(pl.BlockSpec((1, d), lambda i: (0, 0)))
        args.append(b)
    in_specs += [pl.BlockSpec((tm, d), lambda i: (i, 0)),
                 pl.BlockSpec((None, None, 1, d), lambda i: (i * tm // rows_per_batch, kg, 0, 0))]
    args += [x, mods]
    return pl.pallas_call(
        functools.partial(_proj_res_kernel, bias=b is not None),
        grid=(n // tm,),
        in_specs=in_specs,
        out_specs=pl.BlockSpec((tm, d), lambda i: (i, 0)),
        out_shape=jax.ShapeDtypeStruct((n, d), F32),
        compiler_params=_params("parallel"),
        name="proj_residual",
    )(*args)


def _dwconv_kernel(prev_ref, main_ref, next_ref, w_ref, bdw_ref, lg_ref, lb_ref, o_ref,
                   buf_ref, conv_ref):
    i = pl.program_id(1)
    tt, d = main_ref.shape
    halo = prev_ref.shape[0]
    pad = CONV_WIDTH // 2
    buf_ref[0:halo, :] = jnp.where(i > 0, prev_ref[...], 0.0)
    buf_ref[halo:halo + tt, :] = main_ref[...]
    buf_ref[halo + tt:, :] = jnp.where(i < pl.num_programs(1) - 1, next_ref[...], 0.0)

    rc, lc = CONV_ROWS, LANES
    win_rows = rc + 2 * halo
    for r0 in range(0, tt, rc):
        for c0 in range(0, d, lc):
            win = buf_ref[r0:r0 + win_rows, c0:c0 + lc]
            acc = jnp.zeros((rc, lc), F32)
            for res in range(SUBLANES):
                shifted = win if res == 0 else pltpu.roll(win, win_rows - res, axis=0)
                for base in range(0, 2 * halo, SUBLANES):
                    k = base + res - (halo - pad)
                    if 0 <= k < CONV_WIDTH:
                        acc = acc + shifted[base:base + rc] * w_ref[k:k + 1, c0:c0 + lc]
            conv_ref[r0:r0 + rc, c0:c0 + lc] = acc + bdw_ref[:, c0:c0 + lc]

    u = conv_ref[...]
    mu = jnp.mean(u, axis=-1, keepdims=True)
    uc = u - mu
    var = jnp.mean(uc * uc, axis=-1, keepdims=True)
    y = uc * lax.rsqrt(var + EPS) * lg_ref[...] + lb_ref[...]
    o_ref[...] = (y * _sigmoid(y)).astype(o_ref.dtype)


def _dwconv_ln_silu(u, w_dw, b_dw, ln_g, ln_b, batch):
    n, d = u.shape
    t = n // batch
    tt, halo = CONV_TT, CONV_HALO
    nt = t // tt
    hb = tt // halo
    kw = w_dw.shape[0]
    wp = jnp.zeros((32, d), F32).at[:kw].set(w_dw)
    vec = pl.BlockSpec((1, d), lambda b, i: (0, 0))
    return pl.pallas_call(
        _dwconv_kernel,
        grid=(batch, nt),
        in_specs=[
            pl.BlockSpec((halo, d), lambda b, i: (jnp.maximum((b * nt + i) * hb - 1, 0), 0)),
            pl.BlockSpec((tt, d), lambda b, i: (b * nt + i, 0)),
            pl.BlockSpec((halo, d), lambda b, i: (jnp.minimum((b * nt + i + 1) * hb, n // halo - 1), 0)),
            pl.BlockSpec((32, d), lambda b, i: (0, 0)),
            vec, vec, vec,
        ],
        out_specs=pl.BlockSpec((tt, d), lambda b, i: (b * nt + i, 0)),
        out_shape=jax.ShapeDtypeStruct((n, d), BF16),
        scratch_shapes=[pltpu.VMEM((tt + 2 * halo, d), F32), pltpu.VMEM((tt, d), F32)],
        compiler_params=_params("parallel", "arbitrary"),
        name="dwconv_ln_silu",
    )(u, u, u, wp, b_dw, ln_g, ln_b)


def _pool_kernel(prev_ref, x_ref, next_ref, ng_ref, sh_ref, sc_ref, gt_ref, w_ref, b_ref, s_ref,
                 o_ref, hbuf_ref, *, seq_len):
    i = pl.program_id(1)
    tt, d = x_ref.shape
    halo = prev_ref.shape[0]
    ng, sh, sc = ng_ref[...], sh_ref[...], sc_ref[...]
    hbuf_ref[0:halo, :] = jnp.where(i > 0, _modnorm(prev_ref[...], ng, sh, sc), 0.0)
    hbuf_ref[halo:halo + tt, :] = _modnorm(x_ref[...], ng, sh, sc)
    hbuf_ref[halo + tt:, :] = jnp.where(i < pl.num_programs(1) - 1,
                                        _modnorm(next_ref[...], ng, sh, sc), 0.0)
    t = i * tt + lax.broadcasted_iota(jnp.int32, (tt, 1), 0)
    grp = d // len(POOL_WINDOWS)
    for gi, w in enumerate(POOL_WINDOWS):
        cs = slice(gi * grp, (gi + 1) * grp)
        left, right = w // 2, w - w // 2 - 1
        acc = hbuf_ref[halo - left:halo - left + tt, cs]
        for off in range(-left + 1, right + 1):
            acc = acc + hbuf_ref[halo + off:halo + off + tt, cs]
        lo = jnp.maximum(t - left, 0)
        hi = jnp.minimum(t + right, seq_len - 1)
        cnt = (hi - lo + 1).astype(F32)
        dlt = acc / cnt - hbuf_ref[halo:halo + tt, cs]
        y = jnp.dot(dlt.astype(BF16), w_ref[gi], preferred_element_type=F32) + b_ref[:, cs]
        o_ref[:, cs] = x_ref[:, cs] + gt_ref[:, cs] * (y * s_ref[:, cs])


def _pool_sublayer(x, mods, ng, w, b, scale, batch):
    n, d = x.shape
    t = n // batch
    tt, halo = POOL_TT, POOL_HALO
    nt = t // tt
    hb = tt // halo
    vec = pl.BlockSpec((1, d), lambda b_, i: (0, 0))

    def mod(k):
        return pl.BlockSpec((None, None, 1, d), lambda b_, i: (b_, k, 0, 0))

    return pl.pallas_call(
        functools.partial(_pool_kernel, seq_len=t),
        grid=(batch, nt),
        in_specs=[
            pl.BlockSpec((halo, d), lambda b_, i: (jnp.maximum((b_ * nt + i) * hb - 1, 0), 0)),
            pl.BlockSpec((tt, d), lambda b_, i: (b_ * nt + i, 0)),
            pl.BlockSpec((halo, d), lambda b_, i: (jnp.minimum((b_ * nt + i + 1) * hb, n // halo - 1), 0)),
            vec, mod(3), mod(4), mod(5),
            pl.BlockSpec(w.shape, lambda b_, i: (0, 0, 0)),
            vec, vec,
        ],
        out_specs=pl.BlockSpec((tt, d), lambda b_, i: (b_ * nt + i, 0)),
        out_shape=jax.ShapeDtypeStruct((n, d), F32),
        scratch_shapes=[pltpu.VMEM((tt + 2 * halo, d), F32)],
        compiler_params=_params("parallel", "arbitrary"),
        name="pool_sublayer",
    )(x, x, x, ng, mods, mods, mods, w, b, scale)


def _gla_tables(c):
    nl = int(math.log2(c))
    idx = np.arange(c)
    t, r = idx[:, None], idx[None, :]
    stacks, masks = [], []
    for direction in (0, 1):
        fwd = direction == 0
        blocks = [
            (r <= t) if fwd else (r >= t),
            (r > t) if fwd else (r < t),
        ]
        msk = [np.eye(c, dtype=bool) if fwd else np.zeros((c, c), bool)]
        for lvl in range(1, nl + 1):
            n = 1 << lvl
            m = (idx // n) * n + n // 2
            mm = m[:, None]
            upper = (idx >= m)[:, None]
            if fwd:
                blk = np.where(upper, (r >= mm) & (r <= t), (r > t) & (r < mm))
                pair = upper & (idx < m)[None, :]
            else:
                blk = np.where(upper, (r >= mm) & (r < t), (r >= t) & (r < mm))
                pair = (~upper) & (idx >= m)[None, :]
            same = (idx // n)[:, None] == (idx // n)[None, :]
            blocks.append(blk)
            msk.append(pair & same)
        blocks.append(np.ones((SUBLANES, c), bool))
        stack = np.concatenate(blocks, axis=0)
        stacks.append(np.concatenate([stack, stack], axis=1))
        masks.append(np.stack(msk))
    return (np.stack(stacks).astype(np.float32), np.stack(masks).astype(np.float32))


def _gla_kernel(q_ref, k_ref, v_ref, og_ref, r_ref, wa2_ref, ba_ref, ng_ref, wst_ref, msk_ref,
                o_ref, oacc_ref, s_ref, vt_ref, qb_ref, kl_ref, dec_ref, *, chunk):
    t_len, hk = q_ref.shape
    c = chunk
    n_chunks = t_len // c
    n_lvl = msk_ref.shape[1] - 1
    qscale = hk ** -0.5
    nt_dims = (((1,), (1,)), ((), ()))

    def chunk_rows(ci):
        return pl.ds(pl.multiple_of(ci * c, c), c)

    def local(ci, carry):
        rows = chunk_rows(ci)
        q = q_ref[rows, :] * qscale
        k = k_ref[rows, :]
        v = v_ref[rows, :]
        vb = v.astype(BF16)
        vt_ref[ci] = jnp.transpose(v).astype(BF16)
        rr = r_ref[rows, :].astype(BF16)
        dirs = (0, 1)
        z = [jnp.dot(rr, wa2_ref[dr], preferred_element_type=F32) + ba_ref[dr] for dr in dirs]
        f = []
        for dr in dirs:
            g = (jnp.minimum(z[dr], 0.0) - jnp.log1p(jnp.exp(-jnp.abs(z[dr])))) * (1.0 / GLA_GATE_TEMP)
            g_hi = g.astype(BF16)
            g_lo = (g - g_hi.astype(F32)).astype(BF16)
            xs = jnp.dot(wst_ref[dr], jnp.concatenate([g_hi, g_lo], axis=0),
                         preferred_element_type=F32)
            f.append(jnp.exp(xs))

        def blk(dr, b):
            return f[dr][b * c:(b + 1) * c]

        scores = [msk_ref[0, 0] * lax.dot_general(q.astype(BF16), k.astype(BF16), nt_dims,
                                                  preferred_element_type=F32),
                  jnp.zeros((c, c), F32)]
        for lvl in range(1, n_lvl + 1):
            for dr in dirs:
                fl = blk(dr, 1 + lvl)
                scores[dr] = scores[dr] + msk_ref[dr, lvl] * lax.dot_general(
                    (q * fl).astype(BF16), (k * fl).astype(BF16), nt_dims,
                    preferred_element_type=F32)
        oacc_ref[rows, :] = jnp.dot((scores[0] + scores[1]).astype(BF16), vb,
                                    preferred_element_type=F32)
        for dr in dirs:
            qb_ref[dr, rows, :] = (q * blk(dr, 0)).astype(BF16)
            kl_ref[dr, rows, :] = (k * blk(dr, 1)).astype(BF16)
            dec_ref[dr, ci] = f[dr][(2 + n_lvl) * c:]
        return carry

    lax.fori_loop(0, n_chunks, local, 0)

    s_ref[...] = jnp.zeros_like(s_ref)

    def carried(it, carry):
        for direction in (0, 1):
            ci = it if direction == 0 else n_chunks - 1 - it
            rows = chunk_rows(ci)
            state = s_ref[direction]
            oacc_ref[rows, :] += lax.dot_general(qb_ref[direction, rows, :], state.astype(BF16),
                                                 nt_dims, preferred_element_type=F32)
            s_ref[direction] = (state * dec_ref[direction, ci][0:1]
                                + jnp.dot(vt_ref[ci], kl_ref[direction, rows, :],
                                          preferred_element_type=F32))
        return carry

    lax.fori_loop(0, n_chunks, carried, 0, unroll=2)

    def finish(ci, carry):
        rows = chunk_rows(ci)
        o = oacc_ref[rows, :]
        ms = jnp.mean(o * o, axis=-1, keepdims=True)
        y = o * lax.rsqrt(ms + EPS) * ng_ref[...]
        og = og_ref[rows, :]
        o_ref[rows, :] = (y * (og * _sigmoid(og))).astype(o_ref.dtype)
        return carry

    lax.fori_loop(0, n_chunks, finish, 0)


def _gla_scan(proj, r, wa2p, ba, norm_g, batch, hk, hv):
    n = proj.shape[0]
    t = n // batch
    c = GLA_CHUNK
    heads = GLA_HEADS
    wst, msk = _gla_tables(c)
    wst = jnp.asarray(wst, BF16)
    msk = jnp.asarray(msk, F32)
    kq, kk = 0, heads
    kv, kg = (2 * heads * hk) // hv, (2 * heads * hk) // hv + heads
    return pl.pallas_call(
        functools.partial(_gla_kernel, chunk=c),
        grid=(batch, heads),
        in_specs=[
            pl.BlockSpec((t, hk), lambda b, h: (b, kq + h)),
            pl.BlockSpec((t, hk), lambda b, h: (b, kk + h)),
            pl.BlockSpec((t, hv), lambda b, h: (b, kv + h)),
            pl.BlockSpec((t, hv), lambda b, h: (b, kg + h)),
            pl.BlockSpec((t, r.shape[1]), lambda b, h: (b, 0)),
            pl.BlockSpec((2, wa2p.shape[1], hk), lambda b, h: (0, 0, h)),
            pl.BlockSpec((2, 1, hk), lambda b, h: (0, 0, h)),
            pl.BlockSpec((1, hv), lambda b, h: (0, 0)),
            pl.BlockSpec(wst.shape, lambda b, h: (0, 0, 0)),
            pl.BlockSpec(msk.shape, lambda b, h: (0, 0, 0, 0)),
        ],
        out_specs=pl.BlockSpec((t, hv), lambda b, h: (b, h)),
        out_shape=jax.ShapeDtypeStruct((n, heads * hv), BF16),
        scratch_shapes=[
            pltpu.VMEM((t, hv), F32),
            pltpu.VMEM((2, hv, hk), F32),
            pltpu.VMEM((t // c, hv, c), BF16),
            pltpu.VMEM((2, t, hk), BF16),
            pltpu.VMEM((2, t, hk), BF16),
            pltpu.VMEM((2, t // c, SUBLANES, hk), F32),
        ],
        compiler_params=_params("parallel", "arbitrary"),
        name="gla_scan",
    )(proj, proj, proj, proj, r, wa2p, ba, norm_g, wst, msk)


def kernel(x, c, ada_w, ada_b, norm_g, ffn_w_gate, ffn_w_up, ffn_w_down, conv_w_in, conv_b_in,
           conv_w_dw, conv_b_dw, conv_ln_g, conv_ln_b, conv_w_out, conv_b_out, gla_w_in, gla_wa1,
           gla_wa2, gla_ba, gla_norm_g, gla_w_out, pool_w, pool_b, pool_scale, final_g):
    batch, t, d = x.shape
    depth = ada_w.shape[0]
    n = batch * t
    n_mixers = 3

    mods_all = _ada_mods(c, ada_w, ada_b).reshape(depth, batch, N_MOD, 1, d)
    fg = final_g.reshape(1, d)

    xs = x.reshape(n, d)
    for i in range(depth):
        mods = mods_all[i]
        ng = norm_g[i].reshape(3, 1, d)
        xs = _ffn(xs, mods, 0, ng[0], ffn_w_gate, ffn_w_up, ffn_w_down, i, 0, fg, t, final=False)

        kind, j = i % n_mixers, i // n_mixers
        if kind == 0:
            u = _norm_mod_matmul(xs, mods, 3, ng[1], conv_w_in[j].astype(BF16),
                                 conv_b_in[j].reshape(1, 2 * d), t, glu=True)
            a = _dwconv_ln_silu(u, conv_w_dw[j], conv_b_dw[j].reshape(1, d),
                                conv_ln_g[j].reshape(1, d), conv_ln_b[j].reshape(1, d), batch)
            xs = _proj_residual(a, conv_w_out[j].astype(BF16), conv_b_out[j].reshape(1, d),
                                xs, mods, 5, t)
        elif kind == 1:
            dk = gla_wa2.shape[-1]
            rank = gla_wa1.shape[-1]
            hk, hv = dk // GLA_HEADS, d // GLA_HEADS
            wa1p = jnp.zeros((d, LANES), F32)
            wa2p = jnp.zeros((2, LANES, dk), F32)
            for dr in (0, 1):
                wa1p = wa1p.at[:, dr * rank:(dr + 1) * rank].set(gla_wa1[j, dr])
                wa2p = wa2p.at[dr, dr * rank:(dr + 1) * rank].set(gla_wa2[j, dr])
            proj, r = _norm_mod_matmul(xs, mods, 3, ng[1], gla_w_in[j].astype(BF16), None, t,
                                       glu=False, wa=wa1p.astype(BF16))
            y = _gla_scan(proj, r, wa2p.astype(BF16), gla_ba[j].reshape(2, 1, dk),
                          gla_norm_g[j].reshape(1, hv), batch, hk, hv)
            xs = _proj_residual(y, gla_w_out[j].astype(BF16), None, xs, mods, 5, t)
        else:
            xs = _pool_sublayer(xs, mods, ng[1], pool_w[j].astype(BF16), pool_b[j].reshape(1, d),
                                pool_scale[j].reshape(1, d), batch)

        xs = _ffn(xs, mods, 6, ng[2], ffn_w_gate, ffn_w_up, ffn_w_down, i, 1, fg, t,
                  final=(i == depth - 1))
    return xs.reshape(batch, t, d)
```

```python
import functools
import math

import numpy as np
import jax
import jax.numpy as jnp
from jax import lax
from jax.experimental import pallas as pl
from jax.experimental.pallas import tpu as pltpu

F32 = jnp.float32
BF16 = jnp.bfloat16

EPS = 1e-6
N_MOD = 9
CONV_WIDTH = 31
GLA_HEADS = 4
GLA_GATE_TEMP = 16.0
POOL_WINDOWS = (2, 4, 8, 16)

LANES = 128
SUBLANES = 8
VMEM_LIMIT = 56 * 1024 * 1024

ROW_CHUNK = 128
GLA_CHUNK = 128
GLA_SUMS_AFTER = 3
FFN_TM = 1024
FFN_TF = 256
FFN_TN = 512
FFN_ROWS = 256
MM_TM = 1024
MM_TN = 1024
PROJ_TM = 512
PROJ_TN = 512
MM_ROWS = 256
CONV_TT = 128
CONV_HALO = 16
CONV_ROWS = 64
POOL_TT = 256
POOL_HALO = 8


def _params(*sem):
    return pltpu.CompilerParams(dimension_semantics=sem, vmem_limit_bytes=VMEM_LIMIT)


def _sigmoid(x):
    return 1.0 / (1.0 + jnp.exp(-x))


def _modnorm(x, ng, sh, sc):
    ms = jnp.mean(x * x, axis=-1, keepdims=True)
    return (x * lax.rsqrt(ms + EPS)) * (ng * (1.0 + sc)) + sh


def _modnorm_rows(x_ref, h_ref, ng_ref, sh_ref, sc_ref):
    ng, sh, sc = ng_ref[...], sh_ref[...], sc_ref[...]

    def body(r, carry):
        rows = pl.ds(pl.multiple_of(r * ROW_CHUNK, ROW_CHUNK), ROW_CHUNK)
        h_ref[rows, :] = _modnorm(x_ref[rows, :], ng, sh, sc).astype(BF16)
        return carry

    lax.fori_loop(0, x_ref.shape[0] // ROW_CHUNK, body, 0)


def _ada_kernel(c_ref, w_ref, b_ref, o_ref):
    c = c_ref[...]
    ca = (c * _sigmoid(c)).astype(BF16)
    o_ref[0] = jnp.dot(ca, w_ref[0].astype(BF16), preferred_element_type=F32) + b_ref[0]


def _ada_mods(c, ada_w, ada_b):
    depth, d, nd = ada_w.shape
    b = c.shape[0]
    bp = 16
    cp = jnp.zeros((bp, d), F32).at[:b].set(c)
    tn = 1024
    out = pl.pallas_call(
        _ada_kernel,
        grid=(depth, nd // tn),
        in_specs=[
            pl.BlockSpec((bp, d), lambda l, j: (0, 0)),
            pl.BlockSpec((1, d, tn), lambda l, j: (l, 0, j)),
            pl.BlockSpec((1, 1, tn), lambda l, j: (l, 0, j)),
        ],
        out_specs=pl.BlockSpec((1, bp, tn), lambda l, j: (l, 0, j)),
        out_shape=jax.ShapeDtypeStruct((depth, bp, nd), F32),
        compiler_params=_params("parallel", "parallel"),
        name="ada_mods",
    )(cp, ada_w, ada_b.reshape(depth, 1, nd))
    return out[:, :b]


def _ffn_kernel(x_ref, ng_ref, sh_ref, sc_ref, gt_ref, fg_ref, wg_ref, wu_ref, wd_ref,
                wgr_ref, wur_ref, wdr_ref, o_ref, h_ref, *, final, rem):
    j = pl.program_id(1)
    d = x_ref.shape[1]
    half_gate = 0.5 * gt_ref[...]

    def swiglu(g, u):
        return (g * _sigmoid(g) * u).astype(BF16)

    @pl.when(j == 0)
    def _():
        if rem:
            ng, sh, sc = ng_ref[...], sh_ref[...], sc_ref[...]
            wgu = jnp.concatenate([wgr_ref[...].astype(BF16), wur_ref[...].astype(BF16)], axis=1)
            wdr = wdr_ref[...].astype(BF16)
            for r0 in range(0, x_ref.shape[0], 2 * FFN_ROWS):
                pair = [slice(r, r + FFN_ROWS) for r in (r0, r0 + FFN_ROWS)]
                gus = []
                for rows in pair:
                    hb = _modnorm(x_ref[rows, :], ng, sh, sc).astype(BF16)
                    h_ref[rows, :] = hb
                    gus.append(jnp.dot(hb, wgu, preferred_element_type=F32))
                for rows, gu in zip(pair, gus):
                    a = swiglu(gu[:, :rem], gu[:, rem:])
                    o_ref[rows, :] = x_ref[rows, :] + half_gate * jnp.dot(
                        a, wdr, preferred_element_type=F32)
        else:
            _modnorm_rows(x_ref, h_ref, ng_ref, sh_ref, sc_ref)
            o_ref[...] = x_ref[...]

    h = h_ref[...]
    g = jnp.dot(h, wg_ref[...].astype(BF16), preferred_element_type=F32)
    u = jnp.dot(h, wu_ref[...].astype(BF16), preferred_element_type=F32)
    a = swiglu(g, u)
    for c0 in range(0, d, FFN_TN):
        cols = slice(c0, c0 + FFN_TN)
        o_ref[:, cols] += half_gate[:, cols] * jnp.dot(a, wd_ref[:, cols].astype(BF16),
                                                       preferred_element_type=F32)

    if final:
        @pl.when(j == pl.num_programs(1) - 1)
        def _():
            fg = fg_ref[...]

            def body(r, carry):
                rows = pl.ds(pl.multiple_of(r * ROW_CHUNK, ROW_CHUNK), ROW_CHUNK)
                o = o_ref[rows, :]
                ms = jnp.mean(o * o, axis=-1, keepdims=True)
                o_ref[rows, :] = o * lax.rsqrt(ms + EPS) * fg
                return carry

            lax.fori_loop(0, o_ref.shape[0] // ROW_CHUNK, body, 0)


def _ffn(x, mods, k0, ng, wg, wu, wd, layer, slot, final_g, rows_per_batch, final):
    n, d = x.shape
    f = wg.shape[-1]
    tm, tf = min(FFN_TM, rows_per_batch), FFN_TF
    nj = f // tf
    rem = f - nj * tf
    rb = rem if rem else tf
    assert f % rb == 0 and rb % LANES == 0
    last = f // rb - 1
    nt = n // tm
    vec = pl.BlockSpec((1, d), lambda i, j: (0, 0))
    once = pl.Buffered(1)

    def mod(k):
        return pl.BlockSpec((None, None, 1, d), lambda i, j: (i * tm // rows_per_batch, k, 0, 0))

    return pl.pallas_call(
        functools.partial(_ffn_kernel, final=final, rem=rem),
        grid=(n // tm, nj),
        in_specs=[
            pl.BlockSpec((tm, d), lambda i, j: (jnp.minimum(i + jnp.minimum(j, 1), nt - 1), 0)),
            vec, mod(k0), mod(k0 + 1), mod(k0 + 2), vec,
            pl.BlockSpec((None, None, d, tf), lambda i, j: (layer, slot, 0, j)),
            pl.BlockSpec((None, None, d, tf), lambda i, j: (layer, slot, 0, j)),
            pl.BlockSpec((None, None, tf, d), lambda i, j: (layer, slot, j, 0)),
            pl.BlockSpec((None, None, d, rb), lambda i, j: (layer, slot, 0, last), pipeline_mode=once),
            pl.BlockSpec((None, None, d, rb), lambda i, j: (layer, slot, 0, last), pipeline_mode=once),
            pl.BlockSpec((None, None, rb, d), lambda i, j: (layer, slot, last, 0), pipeline_mode=once),
        ],
        out_specs=pl.BlockSpec((tm, d), lambda i, j: (i, 0)),
        out_shape=jax.ShapeDtypeStruct((n, d), F32),
        scratch_shapes=[pltpu.VMEM((tm, d), BF16)],
        compiler_params=_params("parallel", "arbitrary"),
        name="ffn",
    )(x, ng, mods, mods, mods, final_g, wg, wu, wd, wg, wu, wd)


def _nmm_kernel(*refs, glu, bias, aux):
    x_ref, ng_ref, sh_ref, sc_ref = refs[:4]
    pos = 4
    w_ref = refs[pos]; pos += 1
    w2_ref = b_ref = b2_ref = wa_ref = None
    if glu:
        w2_ref = refs[pos]; pos += 1
    if bias:
        b_ref = refs[pos]; pos += 1
        if glu:
            b2_ref = refs[pos]; pos += 1
    if aux:
        wa_ref = refs[pos]; pos += 1
    o_ref = refs[pos]; pos += 1
    r_ref = None
    if aux:
        r_ref = refs[pos]; pos += 1
    h_ref = refs[pos]

    def project(h):
        y = jnp.dot(h, w_ref[...], preferred_element_type=F32)
        if bias:
            y = y + b_ref[...]
        if glu:
            y2 = jnp.dot(h, w2_ref[...], preferred_element_type=F32)
            if bias:
                y2 = y2 + b2_ref[...]
            y = y * _sigmoid(y2)
        return y.astype(o_ref.dtype)

    @pl.when(pl.program_id(1) == 0)
    def _():
        ng, sh, sc = ng_ref[...], sh_ref[...], sc_ref[...]
        for r0 in range(0, x_ref.shape[0], MM_ROWS):
            rows = slice(r0, r0 + MM_ROWS)
            hb = _modnorm(x_ref[rows, :], ng, sh, sc).astype(BF16)
            h_ref[rows, :] = hb
            if aux:
                r_ref[rows, :] = jnp.dot(hb, wa_ref[...], preferred_element_type=F32)
            o_ref[rows, :] = project(hb)

    @pl.when(pl.program_id(1) > 0)
    def _():
        o_ref[...] = project(h_ref[...])


def _norm_mod_matmul(x, mods, k0, ng, w, b, rows_per_batch, *, glu, wa=None):
    n, d = x.shape
    nout = w.shape[1] // 2 if glu else w.shape[1]
    tm, tn = min(MM_TM, rows_per_batch), min(MM_TN, nout)
    nj = nout // tn
    vec = pl.BlockSpec((1, d), lambda i, j: (0, 0))

    def mod(k):
        return pl.BlockSpec((None, None, 1, d), lambda i, j: (i * tm // rows_per_batch, k, 0, 0))

    in_specs = [pl.BlockSpec((tm, d), lambda i, j: (i, 0)), vec, mod(k0), mod(k0 + 1),
                pl.BlockSpec((d, tn), lambda i, j: (0, j))]
    args = [x, ng, mods, mods, w]
    if glu:
        in_specs.append(pl.BlockSpec((d, tn), lambda i, j: (0, j + nj)))
        args.append(w)
    if b is not None:
        in_specs.append(pl.BlockSpec((1, tn), lambda i, j: (0, j)))
        args.append(b)
        if glu:
            in_specs.append(pl.BlockSpec((1, tn), lambda i, j: (0, j + nj)))
            args.append(b)
    out_specs = pl.BlockSpec((tm, tn), lambda i, j: (i, j))
    out_shape = jax.ShapeDtypeStruct((n, nout), F32)
    if wa is not None:
        in_specs.append(pl.BlockSpec(wa.shape, lambda i, j: (0, 0)))
        args.append(wa)
        out_specs = [out_specs, pl.BlockSpec((tm, wa.shape[1]), lambda i, j: (i, 0))]
        out_shape = [out_shape, jax.ShapeDtypeStruct((n, wa.shape[1]), F32)]
    return pl.pallas_call(
        functools.partial(_nmm_kernel, glu=glu, bias=b is not None, aux=wa is not None),
        grid=(n // tm, nj),
        in_specs=in_specs,
        out_specs=out_specs,
        out_shape=out_shape,
        scratch_shapes=[pltpu.VMEM((tm, d), BF16)],
        compiler_params=_params("parallel", "arbitrary"),
        name="norm_mod_matmul",
    )(*args)


def _proj_res_kernel(*refs, bias):
    a_ref, w_ref = refs[:2]
    pos = 2
    b_ref = None
    if bias:
        b_ref = refs[pos]; pos += 1
    x_ref, gt_ref, o_ref = refs[pos:pos + 3]
    a = a_ref[...]
    tn = min(PROJ_TN, o_ref.shape[1])
    for c0 in range(0, o_ref.shape[1], tn):
        cols = slice(c0, c0 + tn)
        y = jnp.dot(a, w_ref[:, cols], preferred_element_type=F32)
        if bias:
            y = y + b_ref[:, cols]
        o_ref[:, cols] = x_ref[:, cols] + gt_ref[:, cols] * y


def _proj_residual(a, w, b, x, mods, kg, rows_per_batch):
    n, kdim = a.shape
    d = w.shape[1]
    tm = min(PROJ_TM, rows_per_batch)
    in_specs = [pl.BlockSpec((tm, kdim), lambda i: (i, 0)),
                pl.BlockSpec((kdim, d), lambda i: (0, 0), pipeline_mode=pl.Buffered(1))]
    args = [a, w]
    if b is not None:
        in_specs.append(pl.BlockSpec((1, d), lambda i: (0, 0)))
        args.append(b)
    in_specs += [pl.BlockSpec((tm, d), lambda i: (i, 0)),
                 pl.BlockSpec((None, None, 1, d), lambda i: (i * tm // rows_per_batch, kg, 0, 0))]
    args += [x, mods]
    return pl.pallas_call(
        functools.partial(_proj_res_kernel, bias=b is not None),
        grid=(n // tm,),
        in_specs=in_specs,
        out_specs=pl.BlockSpec((tm, d), lambda i: (i, 0)),
        out_shape=jax.ShapeDtypeStruct((n, d), F32),
        compiler_params=_params("parallel"),
        name="proj_residual",
    )(*args)


def _dwconv_kernel(prev_ref, main_ref, next_ref, w_ref, bdw_ref, lg_ref, lb_ref, o_ref,
                   buf_ref, conv_ref):
    i = pl.program_id(1)
    tt, d = main_ref.shape
    halo = prev_ref.shape[0]
    pad = CONV_WIDTH // 2
    buf_ref[0:halo, :] = jnp.where(i > 0, prev_ref[...], 0.0)
    buf_ref[halo:halo + tt, :] = main_ref[...]
    buf_ref[halo + tt:, :] = jnp.where(i < pl.num_programs(1) - 1, next_ref[...], 0.0)

    rc, lc = CONV_ROWS, LANES
    win_rows = rc + 2 * halo
    for r0 in range(0, tt, rc):
        for c0 in range(0, d, lc):
            win = buf_ref[r0:r0 + win_rows, c0:c0 + lc]
            acc = jnp.zeros((rc, lc), F32)
            for res in range(SUBLANES):
                shifted = win if res == 0 else pltpu.roll(win, win_rows - res, axis=0)
                for base in range(0, 2 * halo, SUBLANES):
                    k = base + res - (halo - pad)
                    if 0 <= k < CONV_WIDTH:
                        acc = acc + shifted[base:base + rc] * w_ref[k:k + 1, c0:c0 + lc]
            conv_ref[r0:r0 + rc, c0:c0 + lc] = acc + bdw_ref[:, c0:c0 + lc]

    u = conv_ref[...]
    mu = jnp.mean(u, axis=-1, keepdims=True)
    uc = u - mu
    var = jnp.mean(uc * uc, axis=-1, keepdims=True)
    y = uc * lax.rsqrt(var + EPS) * lg_ref[...] + lb_ref[...]
    o_ref[...] = (y * _sigmoid(y)).astype(o_ref.dtype)


def _dwconv_ln_silu(u, w_dw, b_dw, ln_g, ln_b, batch):
    n, d = u.shape
    t = n // batch
    tt, halo = CONV_TT, CONV_HALO
    nt = t // tt
    hb = tt // halo
    kw = w_dw.shape[0]
    wp = jnp.zeros((32, d), F32).at[:kw].set(w_dw)
    vec = pl.BlockSpec((1, d), lambda b, i: (0, 0))
    return pl.pallas_call(
        _dwconv_kernel,
        grid=(batch, nt),
        in_specs=[
            pl.BlockSpec((halo, d), lambda b, i: (jnp.maximum((b * nt + i) * hb - 1, 0), 0)),
            pl.BlockSpec((tt, d), lambda b, i: (b * nt + i, 0)),
            pl.BlockSpec((halo, d), lambda b, i: (jnp.minimum((b * nt + i + 1) * hb, n // halo - 1), 0)),
            pl.BlockSpec((32, d), lambda b, i: (0, 0)),
            vec, vec, vec,
        ],
        out_specs=pl.BlockSpec((tt, d), lambda b, i: (b * nt + i, 0)),
        out_shape=jax.ShapeDtypeStruct((n, d), BF16),
        scratch_shapes=[pltpu.VMEM((tt + 2 * halo, d), F32), pltpu.VMEM((tt, d), F32)],
        compiler_params=_params("parallel", "arbitrary"),
        name="dwconv_ln_silu",
    )(u, u, u, wp, b_dw, ln_g, ln_b)


def _pool_kernel(prev_ref, x_ref, next_ref, ng_ref, sh_ref, sc_ref, gt_ref, w_ref, b_ref, s_ref,
                 o_ref, hbuf_ref, *, seq_len):
    i = pl.program_id(1)
    tt, d = x_ref.shape
    halo = prev_ref.shape[0]
    ng, sh, sc = ng_ref[...], sh_ref[...], sc_ref[...]
    hbuf_ref[0:halo, :] = jnp.where(i > 0, _modnorm(prev_ref[...], ng, sh, sc), 0.0)
    hbuf_ref[halo:halo + tt, :] = _modnorm(x_ref[...], ng, sh, sc)
    hbuf_ref[halo + tt:, :] = jnp.where(i < pl.num_programs(1) - 1,
                                        _modnorm(next_ref[...], ng, sh, sc), 0.0)
    t = i * tt + lax.broadcasted_iota(jnp.int32, (tt, 1), 0)
    grp = d // len(POOL_WINDOWS)
    for gi, w in enumerate(POOL_WINDOWS):
        cs = slice(gi * grp, (gi + 1) * grp)
        left, right = w // 2, w - w // 2 - 1
        acc = hbuf_ref[halo - left:halo - left + tt, cs]
        for off in range(-left + 1, right + 1):
            acc = acc + hbuf_ref[halo + off:halo + off + tt, cs]
        lo = jnp.maximum(t - left, 0)
        hi = jnp.minimum(t + right, seq_len - 1)
        cnt = (hi - lo + 1).astype(F32)
        dlt = acc / cnt - hbuf_ref[halo:halo + tt, cs]
        y = jnp.dot(dlt.astype(BF16), w_ref[gi], preferred_element_type=F32) + b_ref[:, cs]
        o_ref[:, cs] = x_ref[:, cs] + gt_ref[:, cs] * (y * s_ref[:, cs])


def _pool_sublayer(x, mods, ng, w, b, scale, batch):
    n, d = x.shape
    t = n // batch
    tt, halo = POOL_TT, POOL_HALO
    nt = t // tt
    hb = tt // halo
    vec = pl.BlockSpec((1, d), lambda b_, i: (0, 0))

    def mod(k):
        return pl.BlockSpec((None, None, 1, d), lambda b_, i: (b_, k, 0, 0))

    return pl.pallas_call(
        functools.partial(_pool_kernel, seq_len=t),
        grid=(batch, nt),
        in_specs=[
            pl.BlockSpec((halo, d), lambda b_, i: (jnp.maximum((b_ * nt + i) * hb - 1, 0), 0)),
            pl.BlockSpec((tt, d), lambda b_, i: (b_ * nt + i, 0)),
            pl.BlockSpec((halo, d), lambda b_, i: (jnp.minimum((b_ * nt + i + 1) * hb, n // halo - 1), 0)),
            vec, mod(3), mod(4), mod(5),
            pl.BlockSpec(w.shape, lambda b_, i: (0, 0, 0)),
            vec, vec,
        ],
        out_specs=pl.BlockSpec((tt, d), lambda b_, i: (b_ * nt + i, 0)),
        out_shape=jax.ShapeDtypeStruct((n, d), F32),
        scratch_shapes=[pltpu.VMEM((tt + 2 * halo, d), F32)],
        compiler_params=_params("parallel", "arbitrary"),
        name="pool_sublayer",
    )(x, x, x, ng, mods, mods, mods, w, b, scale)


def _gla_tables(c):
    nl = int(math.log2(c))
    idx = np.arange(c)
    t, r = idx[:, None], idx[None, :]
    stacks, masks = [], []
    for direction in (0, 1):
        fwd = direction == 0
        blocks = [
            (r <= t) if fwd else (r >= t),
            (r > t) if fwd else (r < t),
        ]
        msk = [np.eye(c, dtype=bool) if fwd else np.zeros((c, c), bool)]
        for lvl in range(1, nl + 1):
            n = 1 << lvl
            m = (idx // n) * n + n // 2
            mm = m[:, None]
            upper = (idx >= m)[:, None]
            if fwd:
                blk = np.where(upper, (r >= mm) & (r <= t), (r > t) & (r < mm))
                pair = upper & (idx < m)[None, :]
            else:
                blk = np.where(upper, (r >= mm) & (r < t), (r >= t) & (r < mm))
                pair = (~upper) & (idx >= m)[None, :]
            same = (idx // n)[:, None] == (idx // n)[None, :]
            blocks.append(blk)
            msk.append(pair & same)
        blocks.append(np.ones((SUBLANES, c), bool))
        stack = np.concatenate(blocks, axis=0)
        stacks.append(np.concatenate([stack, stack], axis=1))
        masks.append(np.stack(msk))
    return (np.stack(stacks).astype(np.float32), np.stack(masks).astype(np.float32))


def _gla_kernel(q_ref, k_ref, v_ref, og_ref, r_ref, wa2_ref, ba_ref, ng_ref, wst_ref, msk_ref,
                o_ref, oacc_ref, s_ref, vt_ref, qb_ref, kl_ref, dec_ref, fa_ref, fb_ref, *, chunk):
    t_len, hk = q_ref.shape
    c = chunk
    n_chunks = t_len // c
    n_lvl = msk_ref.shape[1] - 1
    qscale = hk ** -0.5
    nt_dims = (((1,), (1,)), ((), ()))

    def chunk_rows(ci):
        return pl.ds(ci * c if isinstance(ci, int) else pl.multiple_of(ci * c, c), c)

    dirs = (0, 1)

    def gate_logits(ci):
        rr = r_ref[chunk_rows(ci), :].astype(BF16)
        return [jnp.dot(rr, wa2_ref[dr], preferred_element_type=F32) + ba_ref[dr] for dr in dirs]

    def range_sums(z):
        xs = []
        for dr in dirs:
            g = (jnp.minimum(z[dr], 0.0) - jnp.log1p(jnp.exp(-jnp.abs(z[dr])))) * (1.0 / GLA_GATE_TEMP)
            g_hi = g.astype(BF16)
            g_lo = (g - g_hi.astype(F32)).astype(BF16)
            xs.append(jnp.dot(wst_ref[dr], jnp.concatenate([g_hi, g_lo], axis=0),
                              preferred_element_type=F32))
        return xs

    def store_factors(xs, f_ref):
        for dr in dirs:
            f_ref[dr] = jnp.exp(xs[dr])

    def local(ci, f_ref, f_next_ref):
        z_next = gate_logits(jnp.minimum(ci + 1, n_chunks - 1))
        rows = chunk_rows(ci)
        q = q_ref[rows, :] * qscale
        k = k_ref[rows, :]
        v = v_ref[rows, :]
        vb = v.astype(BF16)
        vt_ref[ci] = jnp.transpose(v).astype(BF16)

        def blk(dr, b):
            return f_ref[dr, b * c:(b + 1) * c, :]

        scores = [msk_ref[0, 0] * lax.dot_general(q.astype(BF16), k.astype(BF16), nt_dims,
                                                  preferred_element_type=F32),
                  jnp.zeros((c, c), F32)]
        xs_next = None
        for lvl in range(1, n_lvl + 1):
            for dr in dirs:
                fl = blk(dr, 1 + lvl)
                scores[dr] = scores[dr] + msk_ref[dr, lvl] * lax.dot_general(
                    (q * fl).astype(BF16), (k * fl).astype(BF16), nt_dims,
                    preferred_element_type=F32)
            if lvl == min(GLA_SUMS_AFTER, n_lvl):
                xs_next = range_sums(z_next)
        store_factors(xs_next, f_next_ref)
        oacc_ref[rows, :] = jnp.dot((scores[0] + scores[1]).astype(BF16), vb,
                                    preferred_element_type=F32)
        for dr in dirs:
            qb_ref[dr, rows, :] = (q * blk(dr, 0)).astype(BF16)
            kl_ref[dr, rows, :] = (k * blk(dr, 1)).astype(BF16)
            dec_ref[dr, ci] = f_ref[dr, (2 + n_lvl) * c:, :]

    assert n_chunks % 2 == 0

    def local_pair(p, carry):
        local(2 * p, fa_ref, fb_ref)
        local(2 * p + 1, fb_ref, fa_ref)
        return carry

    store_factors(range_sums(gate_logits(0)), fa_ref)
    lax.fori_loop(0, n_chunks // 2, local_pair, 0)

    s_ref[...] = jnp.zeros_like(s_ref)

    def carried(it, carry):
        for direction in (0, 1):
            ci = it if direction == 0 else n_chunks - 1 - it
            rows = chunk_rows(ci)
            state = s_ref[direction]
            oacc_ref[rows, :] += lax.dot_general(qb_ref[direction, rows, :], state.astype(BF16),
                                                 nt_dims, preferred_element_type=F32)
            s_ref[direction] = (state * dec_ref[direction, ci][0:1]
                                + jnp.dot(vt_ref[ci], kl_ref[direction, rows, :],
                                          preferred_element_type=F32))
        return carry

    lax.fori_loop(0, n_chunks, carried, 0, unroll=2)

    def finish(ci, carry):
        rows = chunk_rows(ci)
        o = oacc_ref[rows, :]
        ms = jnp.mean(o * o, axis=-1, keepdims=True)
        y = o * lax.rsqrt(ms + EPS) * ng_ref[...]
        og = og_ref[rows, :]
        o_ref[rows, :] = (y * (og * _sigmoid(og))).astype(o_ref.dtype)
        return carry

    lax.fori_loop(0, n_chunks, finish, 0)


def _gla_scan(proj, r, wa2p, ba, norm_g, batch, hk, hv):
    n = proj.shape[0]
    t = n // batch
    c = GLA_CHUNK
    heads = GLA_HEADS
    wst, msk = _gla_tables(c)
    wst = jnp.asarray(wst, BF16)
    msk = jnp.asarray(msk, F32)
    kq, kk = 0, heads
    kv, kg = (2 * heads * hk) // hv, (2 * heads * hk) // hv + heads
    return pl.pallas_call(
        functools.partial(_gla_kernel, chunk=c),
        grid=(batch, heads),
        in_specs=[
            pl.BlockSpec((t, hk), lambda b, h: (b, kq + h)),
            pl.BlockSpec((t, hk), lambda b, h: (b, kk + h)),
            pl.BlockSpec((t, hv), lambda b, h: (b, kv + h)),
            pl.BlockSpec((t, hv), lambda b, h: (b, kg + h)),
            pl.BlockSpec((t, r.shape[1]), lambda b, h: (b, 0)),
            pl.BlockSpec((2, wa2p.shape[1], hk), lambda b, h: (0, 0, h)),
            pl.BlockSpec((2, 1, hk), lambda b, h: (0, 0, h)),
            pl.BlockSpec((1, hv), lambda b, h: (0, 0)),
            pl.BlockSpec(wst.shape, lambda b, h: (0, 0, 0)),
            pl.BlockSpec(msk.shape, lambda b, h: (0, 0, 0, 0)),
        ],
        out_specs=pl.BlockSpec((t, hv), lambda b, h: (b, h)),
        out_shape=jax.ShapeDtypeStruct((n, heads * hv), BF16),
        scratch_shapes=[
            pltpu.VMEM((t, hv), F32),
            pltpu.VMEM((2, hv, hk), F32),
            pltpu.VMEM((t // c, hv, c), BF16),
            pltpu.VMEM((2, t, hk), BF16),
            pltpu.VMEM((2, t, hk), BF16),
            pltpu.VMEM((2, t // c, SUBLANES, hk), F32),
            pltpu.VMEM((2, wst.shape[1], hk), F32),
            pltpu.VMEM((2, wst.shape[1], hk), F32),
        ],
        compiler_params=_params("parallel", "arbitrary"),
        name="gla_scan",
    )(proj, proj, proj, proj, r, wa2p, ba, norm_g, wst, msk)


def kernel(x, c, ada_w, ada_b, norm_g, ffn_w_gate, ffn_w_up, ffn_w_down, conv_w_in, conv_b_in,
           conv_w_dw, conv_b_dw, conv_ln_g, conv_ln_b, conv_w_out, conv_b_out, gla_w_in, gla_wa1,
           gla_wa2, gla_ba, gla_norm_g, gla_w_out, pool_w, pool_b, pool_scale, final_g):
    batch, t, d = x.shape
    depth = ada_w.shape[0]
    n = batch * t
    n_mixers = 3

    mods_all = _ada_mods(c, ada_w, ada_b).reshape(depth, batch, N_MOD, 1, d)
    fg = final_g.reshape(1, d)

    xs = x.reshape(n, d)
    for i in range(depth):
        mods = mods_all[i]
        ng = norm_g[i].reshape(3, 1, d)
        xs = _ffn(xs, mods, 0, ng[0], ffn_w_gate, ffn_w_up, ffn_w_down, i, 0, fg, t, final=False)

        kind, j = i % n_mixers, i // n_mixers
        if kind == 0:
            u = _norm_mod_matmul(xs, mods, 3, ng[1], conv_w_in[j].astype(BF16),
                                 conv_b_in[j].reshape(1, 2 * d), t, glu=True)
            a = _dwconv_ln_silu(u, conv_w_dw[j], conv_b_dw[j].reshape(1, d),
                                conv_ln_g[j].reshape(1, d), conv_ln_b[j].reshape(1, d), batch)
            xs = _proj_residual(a, conv_w_out[j].astype(BF16), conv_b_out[j].reshape(1, d),
                                xs, mods, 5, t)
        elif kind == 1:
            dk = gla_wa2.shape[-1]
            rank = gla_wa1.shape[-1]
            hk, hv = dk // GLA_HEADS, d // GLA_HEADS
            wa1p = jnp.zeros((d, LANES), F32)
            wa2p = jnp.zeros((2, LANES, dk), F32)
            for dr in (0, 1):
                wa1p = wa1p.at[:, dr * rank:(dr + 1) * rank].set(gla_wa1[j, dr])
                wa2p = wa2p.at[dr, dr * rank:(dr + 1) * rank].set(gla_wa2[j, dr])
            proj, r = _norm_mod_matmul(xs, mods, 3, ng[1], gla_w_in[j].astype(BF16), None, t,
                                       glu=False, wa=wa1p.astype(BF16))
            y = _gla_scan(proj, r, wa2p.astype(BF16), gla_ba[j].reshape(2, 1, dk),
                          gla_norm_g[j].reshape(1, hv), batch, hk, hv)
            xs = _proj_residual(y, gla_w_out[j].astype(BF16), None, xs, mods, 5, t)
        else:
            xs = _pool_sublayer(xs, mods, ng[1], pool_w[j].astype(BF16), pool_b[j].reshape(1, d),
                                pool_scale[j].reshape(1, d), batch)

        xs = _ffn(xs, mods, 6, ng[2], ffn_w_gate, ffn_w_up, ffn_w_down, i, 1, fg, t,
                  final=(i == depth - 1))
    return xs.reshape(batch, t, d)
```

```python
import functools
import math

import numpy as np
import jax
import jax.numpy as jnp
from jax import lax
from jax.experimental import pallas as pl
from jax.experimental.pallas import tpu as pltpu

F32 = jnp.float32
BF16 = jnp.bfloat16

EPS = 1e-6
N_MOD = 9
CONV_WIDTH = 31
GLA_HEADS = 4
GLA_GATE_TEMP = 16.0
POOL_WINDOWS = (2, 4, 8, 16)

LANES = 128
SUBLANES = 8
VMEM_LIMIT = 56 * 1024 * 1024

ROW_CHUNK = 128
GLA_CHUNK = 128
GLA_SUMS_AFTER = 3
FFN_TM = 1024
FFN_TF = 256
FFN_TN = 512
FFN_ROWS = 256
MM_TM = 1024
MM_TN = 1024
PROJ_TM = 512
PROJ_TN = 512
MM_ROWS = 256
CONV_TT = 128
CONV_HALO = 16
CONV_ROWS = 64
POOL_TT = 256
POOL_HALO = 8


def _params(*sem):
    return pltpu.CompilerParams(dimension_semantics=sem, vmem_limit_bytes=VMEM_LIMIT)


def _sigmoid(x):
    return 1.0 / (1.0 + jnp.exp(-x))


def _modnorm(x, ng, sh, sc):
    ms = jnp.mean(x * x, axis=-1, keepdims=True)
    return (x * lax.rsqrt(ms + EPS)) * (ng * (1.0 + sc)) + sh


def _modnorm_rows(x_ref, h_ref, ng_ref, sh_ref, sc_ref):
    ng, sh, sc = ng_ref[...], sh_ref[...], sc_ref[...]

    def body(r, carry):
        rows = pl.ds(pl.multiple_of(r * ROW_CHUNK, ROW_CHUNK), ROW_CHUNK)
        h_ref[rows, :] = _modnorm(x_ref[rows, :], ng, sh, sc).astype(BF16)
        return carry

    lax.fori_loop(0, x_ref.shape[0] // ROW_CHUNK, body, 0)


def _ada_kernel(c_ref, w_ref, b_ref, o_ref):
    c = c_ref[...]
    ca = (c * _sigmoid(c)).astype(BF16)
    o_ref[0] = jnp.dot(ca, w_ref[0].astype(BF16), preferred_element_type=F32) + b_ref[0]


def _ada_mods(c, ada_w, ada_b):
    depth, d, nd = ada_w.shape
    b = c.shape[0]
    bp = 16
    cp = jnp.zeros((bp, d), F32).at[:b].set(c)
    tn = 1024
    out = pl.pallas_call(
        _ada_kernel,
        grid=(depth, nd // tn),
        in_specs=[
            pl.BlockSpec((bp, d), lambda l, j: (0, 0)),
            pl.BlockSpec((1, d, tn), lambda l, j: (l, 0, j)),
            pl.BlockSpec((1, 1, tn), lambda l, j: (l, 0, j)),
        ],
        out_specs=pl.BlockSpec((1, bp, tn), lambda l, j: (l, 0, j)),
        out_shape=jax.ShapeDtypeStruct((depth, bp, nd), F32),
        compiler_params=_params("parallel", "parallel"),
        name="ada_mods",
    )(cp, ada_w, ada_b.reshape(depth, 1, nd))
    return out[:, :b]


def _ffn_kernel(x_ref, ng_ref, sh_ref, sc_ref, gt_ref, fg_ref, wg_ref, wu_ref, wd_ref,
                wgr_ref, wur_ref, wdr_ref, o_ref, h_ref, *, final, rem):
    j = pl.program_id(1)
    d = x_ref.shape[1]
    half_gate = 0.5 * gt_ref[...]

    def swiglu(g, u):
        return (g * _sigmoid(g) * u).astype(BF16)

    @pl.when(j == 0)
    def _():
        if rem:
            ng, sh, sc = ng_ref[...], sh_ref[...], sc_ref[...]
            wgu = jnp.concatenate([wgr_ref[...].astype(BF16), wur_ref[...].astype(BF16)], axis=1)
            wdr = wdr_ref[...].astype(BF16)
            for r0 in range(0, x_ref.shape[0], 2 * FFN_ROWS):
                pair = [slice(r, r + FFN_ROWS) for r in (r0, r0 + FFN_ROWS)]
                gus = []
                for rows in pair:
                    hb = _modnorm(x_ref[rows, :], ng, sh, sc).astype(BF16)
                    h_ref[rows, :] = hb
                    gus.append(jnp.dot(hb, wgu, preferred_element_type=F32))
                for rows, gu in zip(pair, gus):
                    a = swiglu(gu[:, :rem], gu[:, rem:])
                    o_ref[rows, :] = x_ref[rows, :] + half_gate * jnp.dot(
                        a, wdr, preferred_element_type=F32)
        else:
            _modnorm_rows(x_ref, h_ref, ng_ref, sh_ref, sc_ref)
            o_ref[...] = x_ref[...]

    h = h_ref[...]
    g = jnp.dot(h, wg_ref[...].astype(BF16), preferred_element_type=F32)
    u = jnp.dot(h, wu_ref[...].astype(BF16), preferred_element_type=F32)
    a = swiglu(g, u)
    for c0 in range(0, d, FFN_TN):
        cols = slice(c0, c0 + FFN_TN)
        o_ref[:, cols] += half_gate[:, cols] * jnp.dot(a, wd_ref[:, cols].astype(BF16),
                                                       preferred_element_type=F32)

    if final:
        @pl.when(j == pl.num_programs(1) - 1)
        def _():
            fg = fg_ref[...]

            def body(r, carry):
                rows = pl.ds(pl.multiple_of(r * ROW_CHUNK, ROW_CHUNK), ROW_CHUNK)
                o = o_ref[rows, :]
                ms = jnp.mean(o * o, axis=-1, keepdims=True)
                o_ref[rows, :] = o * lax.rsqrt(ms + EPS) * fg
                return carry

            lax.fori_loop(0, o_ref.shape[0] // ROW_CHUNK, body, 0)


def _ffn(x, mods, k0, ng, wg, wu, wd, layer, slot, final_g, rows_per_batch, final):
    n, d = x.shape
    f = wg.shape[-1]
    tm, tf = min(FFN_TM, rows_per_batch), FFN_TF
    nj = f // tf
    rem = f - nj * tf
    rb = rem if rem else tf
    assert f % rb == 0 and rb % LANES == 0 and tm % (2 * FFN_ROWS) == 0
    last = f // rb - 1
    nt = n // tm
    vec = pl.BlockSpec((1, d), lambda i, j: (0, 0))
    once = pl.Buffered(1)

    def mod(k):
        return pl.BlockSpec((None, None, 1, d), lambda i, j: (i * tm // rows_per_batch, k, 0, 0))

    return pl.pallas_call(
        functools.partial(_ffn_kernel, final=final, rem=rem),
        grid=(n // tm, nj),
        in_specs=[
            pl.BlockSpec((tm, d), lambda i, j: (jnp.minimum(i + jnp.minimum(j, 1), nt - 1), 0)),
            vec, mod(k0), mod(k0 + 1), mod(k0 + 2), vec,
            pl.BlockSpec((None, None, d, tf), lambda i, j: (layer, slot, 0, j)),
            pl.BlockSpec((None, None, d, tf), lambda i, j: (layer, slot, 0, j)),
            pl.BlockSpec((None, None, tf, d), lambda i, j: (layer, slot, j, 0)),
            pl.BlockSpec((None, None, d, rb), lambda i, j: (layer, slot, 0, last), pipeline_mode=once),
            pl.BlockSpec((None, None, d, rb), lambda i, j: (layer, slot, 0, last), pipeline_mode=once),
            pl.BlockSpec((None, None, rb, d), lambda i, j: (layer, slot, last, 0), pipeline_mode=once),
        ],
        out_specs=pl.BlockSpec((tm, d), lambda i, j: (i, 0)),
        out_shape=jax.ShapeDtypeStruct((n, d), F32),
        scratch_shapes=[pltpu.VMEM((tm, d), BF16)],
        compiler_params=_params("parallel", "arbitrary"),
        name="ffn",
    )(x, ng, mods, mods, mods, final_g, wg, wu, wd, wg, wu, wd)


def _nmm_kernel(*refs, glu, bias, aux):
    x_ref, ng_ref, sh_ref, sc_ref = refs[:4]
    pos = 4
    w_ref = refs[pos]; pos += 1
    w2_ref = b_ref = b2_ref = wa_ref = None
    if glu:
        w2_ref = refs[pos]; pos += 1
    if bias:
        b_ref = refs[pos]; pos += 1
        if glu:
            b2_ref = refs[pos]; pos += 1
    if aux:
        wa_ref = refs[pos]; pos += 1
    o_ref = refs[pos]; pos += 1
    r_ref = None
    if aux:
        r_ref = refs[pos]; pos += 1
    h_ref = refs[pos]

    def project(h):
        y = jnp.dot(h, w_ref[...], preferred_element_type=F32)
        if bias:
            y = y + b_ref[...]
        if glu:
            y2 = jnp.dot(h, w2_ref[...], preferred_element_type=F32)
            if bias:
                y2 = y2 + b2_ref[...]
            y = y * _sigmoid(y2)
        return y.astype(o_ref.dtype)

    @pl.when(pl.program_id(1) == 0)
    def _():
        ng, sh, sc = ng_ref[...], sh_ref[...], sc_ref[...]
        for r0 in range(0, x_ref.shape[0], MM_ROWS):
            rows = slice(r0, r0 + MM_ROWS)
            hb = _modnorm(x_ref[rows, :], ng, sh, sc).astype(BF16)
            h_ref[rows, :] = hb
            if aux:
                r_ref[rows, :] = jnp.dot(hb, wa_ref[...], preferred_element_type=F32)
            o_ref[rows, :] = project(hb)

    @pl.when(pl.program_id(1) > 0)
    def _():
        o_ref[...] = project(h_ref[...])


def _norm_mod_matmul(x, mods, k0, ng, w, layer, b, rows_per_batch, *, glu, wa=None):
    n, d = x.shape
    nout = w.shape[2] // 2 if glu else w.shape[2]
    tm, tn = min(MM_TM, rows_per_batch), min(MM_TN, nout)
    nj = nout // tn
    vec = pl.BlockSpec((1, d), lambda i, j: (0, 0))

    def mod(k):
        return pl.BlockSpec((None, None, 1, d), lambda i, j: (i * tm // rows_per_batch, k, 0, 0))

    in_specs = [pl.BlockSpec((tm, d), lambda i, j: (i, 0)), vec, mod(k0), mod(k0 + 1),
                pl.BlockSpec((None, d, tn), lambda i, j: (layer, 0, j))]
    args = [x, ng, mods, mods, w]
    if glu:
        in_specs.append(pl.BlockSpec((None, d, tn), lambda i, j: (layer, 0, j + nj)))
        args.append(w)
    if b is not None:
        in_specs.append(pl.BlockSpec((1, tn), lambda i, j: (0, j)))
        args.append(b)
        if glu:
            in_specs.append(pl.BlockSpec((1, tn), lambda i, j: (0, j + nj)))
            args.append(b)
    out_specs = pl.BlockSpec((tm, tn), lambda i, j: (i, j))
    out_shape = jax.ShapeDtypeStruct((n, nout), F32)
    if wa is not None:
        in_specs.append(pl.BlockSpec(wa.shape, lambda i, j: (0, 0)))
        args.append(wa)
        out_specs = [out_specs, pl.BlockSpec((tm, wa.shape[1]), lambda i, j: (i, 0))]
        out_shape = [out_shape, jax.ShapeDtypeStruct((n, wa.shape[1]), F32)]
    return pl.pallas_call(
        functools.partial(_nmm_kernel, glu=glu, bias=b is not None, aux=wa is not None),
        grid=(n // tm, nj),
        in_specs=in_specs,
        out_specs=out_specs,
        out_shape=out_shape,
        scratch_shapes=[pltpu.VMEM((tm, d), BF16)],
        compiler_params=_params("parallel", "arbitrary"),
        name="norm_mod_matmul",
    )(*args)


def _proj_res_kernel(*refs, bias):
    a_ref, w_ref = refs[:2]
    pos = 2
    b_ref = None
    if bias:
        b_ref = refs[pos]; pos += 1
    x_ref, gt_ref, o_ref = refs[pos:pos + 3]
    a = a_ref[...]
    tn = min(PROJ_TN, o_ref.shape[1])
    for c0 in range(0, o_ref.shape[1], tn):
        cols = slice(c0, c0 + tn)
        y = jnp.dot(a, w_ref[:, cols], preferred_element_type=F32)
        if bias:
            y = y + b_ref[:, cols]
        o_ref[:, cols] = x_ref[:, cols] + gt_ref[:, cols] * y


def _proj_residual(a, w, layer, b, x, mods, kg, rows_per_batch):
    n, kdim = a.shape
    d = w.shape[2]
    tm = min(PROJ_TM, rows_per_batch)
    in_specs = [pl.BlockSpec((tm, kdim), lambda i: (i, 0)),
                pl.BlockSpec((None, kdim, d), lambda i: (layer, 0, 0), pipeline_mode=pl.Buffered(1))]
    args = [a, w]
    if b is not None:
        in_specs.append(pl.BlockSpec((1, d), lambda i: (0, 0)))
        args.append(b)
    in_specs += [pl.BlockSpec((tm, d), lambda i: (i, 0)),
                 pl.BlockSpec((None, None, 1, d), lambda i: (i * tm // rows_per_batch, kg, 0, 0))]
    args += [x, mods]
    return pl.pallas_call(
        functools.partial(_proj_res_kernel, bias=b is not None),
        grid=(n // tm,),
        in_specs=in_specs,
        out_specs=pl.BlockSpec((tm, d), lambda i: (i, 0)),
        out_shape=jax.ShapeDtypeStruct((n, d), F32),
        compiler_params=_params("parallel"),
        name="proj_residual",
    )(*args)


def _dwconv_kernel(prev_ref, main_ref, next_ref, w_ref, bdw_ref, lg_ref, lb_ref, o_ref,
                   buf_ref, conv_ref):
    i = pl.program_id(1)
    tt, d = main_ref.shape
    halo = prev_ref.shape[0]
    pad = CONV_WIDTH // 2
    buf_ref[0:halo, :] = jnp.where(i > 0, prev_ref[...], 0.0)
    buf_ref[halo:halo + tt, :] = main_ref[...]
    buf_ref[halo + tt:, :] = jnp.where(i < pl.num_programs(1) - 1, next_ref[...], 0.0)

    rc, lc = CONV_ROWS, LANES
    win_rows = rc + 2 * halo
    for r0 in range(0, tt, rc):
        for c0 in range(0, d, lc):
            win = buf_ref[r0:r0 + win_rows, c0:c0 + lc]
            acc = jnp.zeros((rc, lc), F32)
            for res in range(SUBLANES):
                shifted = win if res == 0 else pltpu.roll(win, win_rows - res, axis=0)
                for base in range(0, 2 * halo, SUBLANES):
                    k = base + res - (halo - pad)
                    if 0 <= k < CONV_WIDTH:
                        acc = acc + shifted[base:base + rc] * w_ref[k:k + 1, c0:c0 + lc]
            conv_ref[r0:r0 + rc, c0:c0 + lc] = acc + bdw_ref[:, c0:c0 + lc]

    u = conv_ref[...]
    mu = jnp.mean(u, axis=-1, keepdims=True)
    uc = u - mu
    var = jnp.mean(uc * uc, axis=-1, keepdims=True)
    y = uc * lax.rsqrt(var + EPS) * lg_ref[...] + lb_ref[...]
    o_ref[...] = (y * _sigmoid(y)).astype(o_ref.dtype)


def _dwconv_ln_silu(u, w_dw, b_dw, ln_g, ln_b, batch):
    n, d = u.shape
    t = n // batch
    tt, halo = CONV_TT, CONV_HALO
    nt = t // tt
    hb = tt // halo
    kw = w_dw.shape[0]
    wp = jnp.zeros((32, d), F32).at[:kw].set(w_dw)
    vec = pl.BlockSpec((1, d), lambda b, i: (0, 0))
    return pl.pallas_call(
        _dwconv_kernel,
        grid=(batch, nt),
        in_specs=[
            pl.BlockSpec((halo, d), lambda b, i: (jnp.maximum((b * nt + i) * hb - 1, 0), 0)),
            pl.BlockSpec((tt, d), lambda b, i: (b * nt + i, 0)),
            pl.BlockSpec((halo, d), lambda b, i: (jnp.minimum((b * nt + i + 1) * hb, n // halo - 1), 0)),
            pl.BlockSpec((32, d), lambda b, i: (0, 0)),
            vec, vec, vec,
        ],
        out_specs=pl.BlockSpec((tt, d), lambda b, i: (b * nt + i, 0)),
        out_shape=jax.ShapeDtypeStruct((n, d), BF16),
        scratch_shapes=[pltpu.VMEM((tt + 2 * halo, d), F32), pltpu.VMEM((tt, d), F32)],
        compiler_params=_params("parallel", "arbitrary"),
        name="dwconv_ln_silu",
    )(u, u, u, wp, b_dw, ln_g, ln_b)


def _pool_kernel(prev_ref, x_ref, next_ref, ng_ref, sh_ref, sc_ref, gt_ref, w_ref, b_ref, s_ref,
                 o_ref, hbuf_ref, *, seq_len):
    i = pl.program_id(1)
    tt, d = x_ref.shape
    halo = prev_ref.shape[0]
    ng, sh, sc = ng_ref[...], sh_ref[...], sc_ref[...]
    hbuf_ref[0:halo, :] = jnp.where(i > 0, _modnorm(prev_ref[...], ng, sh, sc), 0.0)
    hbuf_ref[halo:halo + tt, :] = _modnorm(x_ref[...], ng, sh, sc)
    hbuf_ref[halo + tt:, :] = jnp.where(i < pl.num_programs(1) - 1,
                                        _modnorm(next_ref[...], ng, sh, sc), 0.0)
    t = i * tt + lax.broadcasted_iota(jnp.int32, (tt, 1), 0)
    grp = d // len(POOL_WINDOWS)
    for gi, w in enumerate(POOL_WINDOWS):
        cs = slice(gi * grp, (gi + 1) * grp)
        left, right = w // 2, w - w // 2 - 1
        assert w == 2 * left and left & (left - 1) == 0 and left <= halo
        rows_all = tt + 2 * halo
        run = hbuf_ref[:, cs]
        span = 1
        while span < left:
            run = run + pltpu.roll(run, rows_all - span, axis=0)
            span *= 2
        if left % SUBLANES == 0:
            acc = run[halo - left:halo - left + tt] + run[halo:halo + tt]
        else:
            acc = (pltpu.roll(run, left, axis=0) + run)[halo:halo + tt]
        lo = jnp.maximum(t - left, 0)
        hi = jnp.minimum(t + right, seq_len - 1)
        cnt = (hi - lo + 1).astype(F32)
        dlt = acc / cnt - hbuf_ref[halo:halo + tt, cs]
        y = jnp.dot(dlt.astype(BF16), w_ref[gi], preferred_element_type=F32) + b_ref[:, cs]
        o_ref[:, cs] = x_ref[:, cs] + gt_ref[:, cs] * (y * s_ref[:, cs])


def _pool_sublayer(x, mods, ng, w, b, scale, batch):
    n, d = x.shape
    t = n // batch
    tt, halo = POOL_TT, POOL_HALO
    nt = t // tt
    hb = tt // halo
    vec = pl.BlockSpec((1, d), lambda b_, i: (0, 0))

    def mod(k):
        return pl.BlockSpec((None, None, 1, d), lambda b_, i: (b_, k, 0, 0))

    return pl.pallas_call(
        functools.partial(_pool_kernel, seq_len=t),
        grid=(batch, nt),
        in_specs=[
            pl.BlockSpec((halo, d), lambda b_, i: (jnp.maximum((b_ * nt + i) * hb - 1, 0), 0)),
            pl.BlockSpec((tt, d), lambda b_, i: (b_ * nt + i, 0)),
            pl.BlockSpec((halo, d), lambda b_, i: (jnp.minimum((b_ * nt + i + 1) * hb, n // halo - 1), 0)),
            vec, mod(3), mod(4), mod(5),
            pl.BlockSpec(w.shape, lambda b_, i: (0, 0, 0)),
            vec, vec,
        ],
        out_specs=pl.BlockSpec((tt, d), lambda b_, i: (b_ * nt + i, 0)),
        out_shape=jax.ShapeDtypeStruct((n, d), F32),
        scratch_shapes=[pltpu.VMEM((tt + 2 * halo, d), F32)],
        compiler_params=_params("parallel", "arbitrary"),
        name="pool_sublayer",
    )(x, x, x, ng, mods, mods, mods, w, b, scale)


def _gla_tables(c):
    nl = int(math.log2(c))
    idx = np.arange(c)
    t, r = idx[:, None], idx[None, :]
    stacks, masks = [], []
    for direction in (0, 1):
        fwd = direction == 0
        blocks = [
            (r <= t) if fwd else (r >= t),
            (r > t) if fwd else (r < t),
        ]
        msk = [np.eye(c, dtype=bool) if fwd else np.zeros((c, c), bool)]
        for lvl in range(1, nl + 1):
            n = 1 << lvl
            m = (idx // n) * n + n // 2
            mm = m[:, None]
            upper = (idx >= m)[:, None]
            if fwd:
                blk = np.where(upper, (r >= mm) & (r <= t), (r > t) & (r < mm))
                pair = upper & (idx < m)[None, :]
            else:
                blk = np.where(upper, (r >= mm) & (r < t), (r >= t) & (r < mm))
                pair = (~upper) & (idx >= m)[None, :]
            same = (idx // n)[:, None] == (idx // n)[None, :]
            blocks.append(blk)
            msk.append(pair & same)
        blocks.append(np.ones((SUBLANES, c), bool))
        stack = np.concatenate(blocks, axis=0)
        stacks.append(np.concatenate([stack, stack], axis=1))
        masks.append(np.stack(msk))
    return (np.stack(stacks).astype(np.float32), np.stack(masks).astype(np.float32))


def _gla_kernel(q_ref, k_ref, v_ref, og_ref, r_ref, wa2_ref, ba_ref, ng_ref, wst_ref, msk_ref,
                o_ref, oacc_ref, s_ref, vt_ref, qb_ref, kl_ref, dec_ref, fa_ref, fb_ref, *, chunk):
    t_len, hk = q_ref.shape
    c = chunk
    n_chunks = t_len // c
    n_lvl = msk_ref.shape[1] - 1
    qscale = hk ** -0.5
    nt_dims = (((1,), (1,)), ((), ()))

    def chunk_rows(ci):
        return pl.ds(ci * c if isinstance(ci, int) else pl.multiple_of(ci * c, c), c)

    dirs = (0, 1)

    def gate_logits(ci):
        rr = r_ref[chunk_rows(ci), :].astype(BF16)
        return [jnp.dot(rr, wa2_ref[dr], preferred_element_type=F32) + ba_ref[dr] for dr in dirs]

    def range_sums(z):
        xs = []
        for dr in dirs:
            g = (jnp.minimum(z[dr], 0.0) - jnp.log1p(jnp.exp(-jnp.abs(z[dr])))) * (1.0 / GLA_GATE_TEMP)
            g_hi = g.astype(BF16)
            g_lo = (g - g_hi.astype(F32)).astype(BF16)
            xs.append(jnp.dot(wst_ref[dr], jnp.concatenate([g_hi, g_lo], axis=0),
                              preferred_element_type=F32))
        return xs

    def store_factors(xs, f_ref):
        for dr in dirs:
            f_ref[dr] = jnp.exp(xs[dr])

    def local(ci, f_ref, f_next_ref):
        z_next = gate_logits(jnp.minimum(ci + 1, n_chunks - 1))
        rows = chunk_rows(ci)
        q = q_ref[rows, :] * qscale
        k = k_ref[rows, :]
        v = v_ref[rows, :]
        vb = v.astype(BF16)
        vt_ref[ci] = jnp.transpose(v).astype(BF16)

        def blk(dr, b):
            return f_ref[dr, b * c:(b + 1) * c, :]

        scores = [msk_ref[0, 0] * lax.dot_general(q.astype(BF16), k.astype(BF16), nt_dims,
                                                  preferred_element_type=F32),
                  jnp.zeros((c, c), F32)]
        xs_next = None
        for lvl in range(1, n_lvl + 1):
            for dr in dirs:
                fl = blk(dr, 1 + lvl)
                scores[dr] = scores[dr] + msk_ref[dr, lvl] * lax.dot_general(
                    (q * fl).astype(BF16), (k * fl).astype(BF16), nt_dims,
                    preferred_element_type=F32)
            if lvl == min(GLA_SUMS_AFTER, n_lvl):
                xs_next = range_sums(z_next)
        store_factors(xs_next, f_next_ref)
        oacc_ref[rows, :] = jnp.dot((scores[0] + scores[1]).astype(BF16), vb,
                                    preferred_element_type=F32)
        for dr in dirs:
            qb_ref[dr, rows, :] = (q * blk(dr, 0)).astype(BF16)
            kl_ref[dr, rows, :] = (k * blk(dr, 1)).astype(BF16)
            dec_ref[dr, ci] = f_ref[dr, (2 + n_lvl) * c:, :]

    assert n_chunks % 2 == 0

    def local_pair(p, carry):
        local(2 * p, fa_ref, fb_ref)
        local(2 * p + 1, fb_ref, fa_ref)
        return carry

    store_factors(range_sums(gate_logits(0)), fa_ref)
    lax.fori_loop(0, n_chunks // 2, local_pair, 0)

    s_ref[...] = jnp.zeros_like(s_ref)

    def carried(it, carry):
        for direction in (0, 1):
            ci = it if direction == 0 else n_chunks - 1 - it
            rows = chunk_rows(ci)
            state = s_ref[direction]
            oacc_ref[rows, :] += lax.dot_general(qb_ref[direction, rows, :], state.astype(BF16),
                                                 nt_dims, preferred_element_type=F32)
            s_ref[direction] = (state * dec_ref[direction, ci][0:1]
                                + jnp.dot(vt_ref[ci], kl_ref[direction, rows, :],
                                          preferred_element_type=F32))
        return carry

    lax.fori_loop(0, n_chunks, carried, 0, unroll=4)

    def finish(ci, carry):
        rows = chunk_rows(ci)
        o = oacc_ref[rows, :]
        ms = jnp.mean(o * o, axis=-1, keepdims=True)
        y = o * lax.rsqrt(ms + EPS) * ng_ref[...]
        og = og_ref[rows, :]
        o_ref[rows, :] = (y * (og * _sigmoid(og))).astype(o_ref.dtype)
        return carry

    lax.fori_loop(0, n_chunks, finish, 0)


def _gla_scan(proj, r, wa2p, ba, norm_g, batch, hk, hv):
    n = proj.shape[0]
    t = n // batch
    c = GLA_CHUNK
    heads = GLA_HEADS
    wst, msk = _gla_tables(c)
    wst = jnp.asarray(wst, BF16)
    msk = jnp.asarray(msk, F32)
    kq, kk = 0, heads
    kv, kg = (2 * heads * hk) // hv, (2 * heads * hk) // hv + heads
    return pl.pallas_call(
        functools.partial(_gla_kernel, chunk=c),
        grid=(batch, heads),
        in_specs=[
            pl.BlockSpec((t, hk), lambda b, h: (b, kq + h)),
            pl.BlockSpec((t, hk), lambda b, h: (b, kk + h)),
            pl.BlockSpec((t, hv), lambda b, h: (b, kv + h)),
            pl.BlockSpec((t, hv), lambda b, h: (b, kg + h)),
            pl.BlockSpec((t, r.shape[1]), lambda b, h: (b, 0)),
            pl.BlockSpec((2, wa2p.shape[1], hk), lambda b, h: (0, 0, h)),
            pl.BlockSpec((2, 1, hk), lambda b, h: (0, 0, h)),
            pl.BlockSpec((1, hv), lambda b, h: (0, 0)),
            pl.BlockSpec(wst.shape, lambda b, h: (0, 0, 0)),
            pl.BlockSpec(msk.shape, lambda b, h: (0, 0, 0, 0)),
        ],
        out_specs=pl.BlockSpec((t, hv), lambda b, h: (b, h)),
        out_shape=jax.ShapeDtypeStruct((n, heads * hv), BF16),
        scratch_shapes=[
            pltpu.VMEM((t, hv), F32),
            pltpu.VMEM((2, hv, hk), F32),
            pltpu.VMEM((t // c, hv, c), BF16),
            pltpu.VMEM((2, t, hk), BF16),
            pltpu.VMEM((2, t, hk), BF16),
            pltpu.VMEM((2, t // c, SUBLANES, hk), F32),
            pltpu.VMEM((2, wst.shape[1], hk), F32),
            pltpu.VMEM((2, wst.shape[1], hk), F32),
        ],
        compiler_params=_params("parallel", "arbitrary"),
        name="gla_scan",
    )(proj, proj, proj, proj, r, wa2p, ba, norm_g, wst, msk)


def kernel(x, c, ada_w, ada_b, norm_g, ffn_w_gate, ffn_w_up, ffn_w_down, conv_w_in, conv_b_in,
           conv_w_dw, conv_b_dw, conv_ln_g, conv_ln_b, conv_w_out, conv_b_out, gla_w_in, gla_wa1,
           gla_wa2, gla_ba, gla_norm_g, gla_w_out, pool_w, pool_b, pool_scale, final_g):
    batch, t, d = x.shape
    depth = ada_w.shape[0]
    n = batch * t
    n_mixers = 3

    mods_all = _ada_mods(c, ada_w, ada_b).reshape(depth, batch, N_MOD, 1, d)
    fg = final_g.reshape(1, d)
    conv_w_in_bf, conv_w_out_bf = conv_w_in.astype(BF16), conv_w_out.astype(BF16)
    gla_w_in_bf, gla_w_out_bf = gla_w_in.astype(BF16), gla_w_out.astype(BF16)

    xs = x.reshape(n, d)
    for i in range(depth):
        mods = mods_all[i]
        ng = norm_g[i].reshape(3, 1, d)
        xs = _ffn(xs, mods, 0, ng[0], ffn_w_gate, ffn_w_up, ffn_w_down, i, 0, fg, t, final=False)

        kind, j = i % n_mixers, i // n_mixers
        if kind == 0:
            u = _norm_mod_matmul(xs, mods, 3, ng[1], conv_w_in_bf, j,
                                 conv_b_in[j].reshape(1, 2 * d), t, glu=True)
            a = _dwconv_ln_silu(u, conv_w_dw[j], conv_b_dw[j].reshape(1, d),
                                conv_ln_g[j].reshape(1, d), conv_ln_b[j].reshape(1, d), batch)
            xs = _proj_residual(a, conv_w_out_bf, j, conv_b_out[j].reshape(1, d),
                                xs, mods, 5, t)
        elif kind == 1:
            dk = gla_wa2.shape[-1]
            rank = gla_wa1.shape[-1]
            hk, hv = dk // GLA_HEADS, d // GLA_HEADS
            wa1p = jnp.zeros((d, LANES), F32)
            wa2p = jnp.zeros((2, LANES, dk), F32)
            for dr in (0, 1):
                wa1p = wa1p.at[:, dr * rank:(dr + 1) * rank].set(gla_wa1[j, dr])
                wa2p = wa2p.at[dr, dr * rank:(dr + 1) * rank].set(gla_wa2[j, dr])
            proj, r = _norm_mod_matmul(xs, mods, 3, ng[1], gla_w_in_bf, j, None, t,
                                       glu=False, wa=wa1p.astype(BF16))
            y = _gla_scan(proj, r, wa2p.astype(BF16), gla_ba[j].reshape(2, 1, dk),
                          gla_norm_g[j].reshape(1, hv), batch, hk, hv)
            xs = _proj_residual(y, gla_w_out_bf, j, None, xs, mods, 5, t)
        else:
            xs = _pool_sublayer(xs, mods, ng[1], pool_w[j].astype(BF16), pool_b[j].reshape(1, d),
                                pool_scale[j].reshape(1, d), batch)

        xs = _ffn(xs, mods, 6, ng[2], ffn_w_gate, ffn_w_up, ffn_w_down, i, 1, fg, t,
                  final=(i == depth - 1))
    return xs.reshape(batch, t, d)
```

```python
import functools
import math

import numpy as np
import jax
import jax.numpy as jnp
from jax import lax
from jax.experimental import pallas as pl
from jax.experimental.pallas import tpu as pltpu

F32 = jnp.float32
BF16 = jnp.bfloat16

EPS = 1e-6
N_MOD = 9
CONV_WIDTH = 31
GLA_HEADS = 4
GLA_GATE_TEMP = 16.0
POOL_WINDOWS = (2, 4, 8, 16)

LANES = 128
SUBLANES = 8
BF16_SUBLANES = 16
VMEM_LIMIT = 56 * 1024 * 1024

ROW_CHUNK = 128
ADA_TN = 1024
GLA_CHUNK = 128
GLA_SUMS_AFTER = 3
FFN_TM = 1024
FFN_TF = 256
FFN_TN = 512
FFN_ROWS = 256
MM_TM = 1024
MM_TN = 1024
PROJ_TM = 512
PROJ_TN = 512
MM_ROWS = 256
CONV_TT = 128
CONV_HALO = 16
CONV_ROWS = 64
POOL_TT = 256
POOL_HALO = 8


def _params(*sem):
    return pltpu.CompilerParams(dimension_semantics=sem, vmem_limit_bytes=VMEM_LIMIT)


def _sigmoid(x):
    return 1.0 / (1.0 + jnp.exp(-x))


def _modnorm(x, ng, sh, sc):
    ms = jnp.mean(x * x, axis=-1, keepdims=True)
    return (x * lax.rsqrt(ms + EPS)) * (ng * (1.0 + sc)) + sh


def _modnorm_rows(x_ref, h_ref, ng_ref, sh_ref, sc_ref):
    ng, sh, sc = ng_ref[...], sh_ref[...], sc_ref[...]

    def body(r, carry):
        rows = pl.ds(pl.multiple_of(r * ROW_CHUNK, ROW_CHUNK), ROW_CHUNK)
        h_ref[rows, :] = _modnorm(x_ref[rows, :], ng, sh, sc).astype(BF16)
        return carry

    lax.fori_loop(0, x_ref.shape[0] // ROW_CHUNK, body, 0)


def _ada_kernel(c_ref, w_ref, b_ref, o_ref):
    c = c_ref[...]
    ca = (c * _sigmoid(c)).astype(BF16)
    o_ref[0] = jnp.dot(ca, w_ref[0].astype(BF16), preferred_element_type=F32) + b_ref[0]


def _ada_mods(c, ada_w, ada_b):
    depth, d, nd = ada_w.shape
    b = c.shape[0]
    bp = -(-b // BF16_SUBLANES) * BF16_SUBLANES
    cp = jnp.zeros((bp, d), F32).at[:b].set(c)
    tn = min(ADA_TN, nd)
    out = pl.pallas_call(
        _ada_kernel,
        grid=(depth, nd // tn),
        in_specs=[
            pl.BlockSpec((bp, d), lambda l, j: (0, 0)),
            pl.BlockSpec((1, d, tn), lambda l, j: (l, 0, j)),
            pl.BlockSpec((1, 1, tn), lambda l, j: (l, 0, j)),
        ],
        out_specs=pl.BlockSpec((1, bp, tn), lambda l, j: (l, 0, j)),
        out_shape=jax.ShapeDtypeStruct((depth, bp, nd), F32),
        compiler_params=_params("parallel", "parallel"),
        name="ada_mods",
    )(cp, ada_w, ada_b.reshape(depth, 1, nd))
    return out[:, :b]


def _ffn_kernel(x_ref, ng_ref, sh_ref, sc_ref, gt_ref, fg_ref, wg_ref, wu_ref, wd_ref,
                wgr_ref, wur_ref, wdr_ref, o_ref, h_ref, *, final, rem):
    j = pl.program_id(1)
    d = x_ref.shape[1]
    half_gate = 0.5 * gt_ref[...]

    def swiglu(g, u):
        return (g * _sigmoid(g) * u).astype(BF16)

    @pl.when(j == 0)
    def _():
        if rem:
            ng, sh, sc = ng_ref[...], sh_ref[...], sc_ref[...]
            wgu = jnp.concatenate([wgr_ref[...].astype(BF16), wur_ref[...].astype(BF16)], axis=1)
            wdr = wdr_ref[...].astype(BF16)
            for r0 in range(0, x_ref.shape[0], 2 * FFN_ROWS):
                pair = [slice(r, r + FFN_ROWS) for r in (r0, r0 + FFN_ROWS)]
                gus = []
                for rows in pair:
                    hb = _modnorm(x_ref[rows, :], ng, sh, sc).astype(BF16)
                    h_ref[rows, :] = hb
                    gus.append(jnp.dot(hb, wgu, preferred_element_type=F32))
                for rows, gu in zip(pair, gus):
                    a = swiglu(gu[:, :rem], gu[:, rem:])
                    o_ref[rows, :] = x_ref[rows, :] + half_gate * jnp.dot(
                        a, wdr, preferred_element_type=F32)
        else:
            _modnorm_rows(x_ref, h_ref, ng_ref, sh_ref, sc_ref)
            o_ref[...] = x_ref[...]

    h = h_ref[...]
    g = jnp.dot(h, wg_ref[...].astype(BF16), preferred_element_type=F32)
    u = jnp.dot(h, wu_ref[...].astype(BF16), preferred_element_type=F32)
    a = swiglu(g, u)
    for c0 in range(0, d, FFN_TN):
        cols = slice(c0, c0 + FFN_TN)
        o_ref[:, cols] += half_gate[:, cols] * jnp.dot(a, wd_ref[:, cols].astype(BF16),
                                                       preferred_element_type=F32)

    if final:
        @pl.when(j == pl.num_programs(1) - 1)
        def _():
            fg = fg_ref[...]

            def body(r, carry):
                rows = pl.ds(pl.multiple_of(r * ROW_CHUNK, ROW_CHUNK), ROW_CHUNK)
                o = o_ref[rows, :]
                ms = jnp.mean(o * o, axis=-1, keepdims=True)
                o_ref[rows, :] = o * lax.rsqrt(ms + EPS) * fg
                return carry

            lax.fori_loop(0, o_ref.shape[0] // ROW_CHUNK, body, 0)


def _ffn(x, mods, k0, ng, wg, wu, wd, layer, slot, final_g, rows_per_batch, final):
    n, d = x.shape
    f = wg.shape[-1]
    tm, tf = min(FFN_TM, rows_per_batch), FFN_TF
    nj = f // tf
    rem = f - nj * tf
    rb = rem if rem else tf
    assert f % rb == 0 and rb % LANES == 0 and tm % (2 * FFN_ROWS) == 0
    last = f // rb - 1
    nt = n // tm
    vec = pl.BlockSpec((1, d), lambda i, j: (0, 0))
    once = pl.Buffered(1)

    def mod(k):
        return pl.BlockSpec((None, None, 1, d), lambda i, j: (i * tm // rows_per_batch, k, 0, 0))

    return pl.pallas_call(
        functools.partial(_ffn_kernel, final=final, rem=rem),
        grid=(n // tm, nj),
        in_specs=[
            pl.BlockSpec((tm, d), lambda i, j: (jnp.minimum(i + jnp.minimum(j, 1), nt - 1), 0)),
            vec, mod(k0), mod(k0 + 1), mod(k0 + 2), vec,
            pl.BlockSpec((None, None, d, tf), lambda i, j: (layer, slot, 0, j)),
            pl.BlockSpec((None, None, d, tf), lambda i, j: (layer, slot, 0, j)),
            pl.BlockSpec((None, None, tf, d), lambda i, j: (layer, slot, j, 0)),
            pl.BlockSpec((None, None, d, rb), lambda i, j: (layer, slot, 0, last), pipeline_mode=once),
            pl.BlockSpec((None, None, d, rb), lambda i, j: (layer, slot, 0, last), pipeline_mode=once),
            pl.BlockSpec((None, None, rb, d), lambda i, j: (layer, slot, last, 0), pipeline_mode=once),
        ],
        out_specs=pl.BlockSpec((tm, d), lambda i, j: (i, 0)),
        out_shape=jax.ShapeDtypeStruct((n, d), F32),
        scratch_shapes=[pltpu.VMEM((tm, d), BF16)],
        compiler_params=_params("parallel", "arbitrary"),
        name="ffn",
    )(x, ng, mods, mods, mods, final_g, wg, wu, wd, wg, wu, wd)


def _nmm_kernel(*refs, glu, bias, aux):
    x_ref, ng_ref, sh_ref, sc_ref = refs[:4]
    pos = 4
    w_ref = refs[pos]; pos += 1
    w2_ref = b_ref = b2_ref = wa_ref = None
    if glu:
        w2_ref = refs[pos]; pos += 1
    if bias:
        b_ref = refs[pos]; pos += 1
        if glu:
            b2_ref = refs[pos]; pos += 1
    if aux:
        wa_ref = refs[pos]; pos += 1
    o_ref = refs[pos]; pos += 1
    r_ref = None
    if aux:
        r_ref = refs[pos]; pos += 1
    h_ref = refs[pos]

    def project(h):
        y = jnp.dot(h, w_ref[...], preferred_element_type=F32)
        if bias:
            y = y + b_ref[...]
        if glu:
            y2 = jnp.dot(h, w2_ref[...], preferred_element_type=F32)
            if bias:
                y2 = y2 + b2_ref[...]
            y = y * _sigmoid(y2)
        return y.astype(o_ref.dtype)

    @pl.when(pl.program_id(1) == 0)
    def _():
        ng, sh, sc = ng_ref[...], sh_ref[...], sc_ref[...]
        for r0 in range(0, x_ref.shape[0], MM_ROWS):
            rows = slice(r0, r0 + MM_ROWS)
            hb = _modnorm(x_ref[rows, :], ng, sh, sc).astype(BF16)
            h_ref[rows, :] = hb
            if aux:
                r_ref[rows, :] = jnp.dot(hb, wa_ref[...], preferred_element_type=F32)
            o_ref[rows, :] = project(hb)

    @pl.when(pl.program_id(1) > 0)
    def _():
        o_ref[...] = project(h_ref[...])


def _norm_mod_matmul(x, mods, k0, ng, w, layer, b, rows_per_batch, *, glu, wa=None):
    n, d = x.shape
    nout = w.shape[2] // 2 if glu else w.shape[2]
    tm, tn = min(MM_TM, rows_per_batch), min(MM_TN, nout)
    nj = nout // tn
    vec = pl.BlockSpec((1, d), lambda i, j: (0, 0))

    def mod(k):
        return pl.BlockSpec((None, None, 1, d), lambda i, j: (i * tm // rows_per_batch, k, 0, 0))

    in_specs = [pl.BlockSpec((tm, d), lambda i, j: (i, 0)), vec, mod(k0), mod(k0 + 1),
                pl.BlockSpec((None, d, tn), lambda i, j: (layer, 0, j))]
    args = [x, ng, mods, mods, w]
    if glu:
        in_specs.append(pl.BlockSpec((None, d, tn), lambda i, j: (layer, 0, j + nj)))
        args.append(w)
    if b is not None:
        in_specs.append(pl.BlockSpec((1, tn), lambda i, j: (0, j)))
        args.append(b)
        if glu:
            in_specs.append(pl.BlockSpec((1, tn), lambda i, j: (0, j + nj)))
            args.append(b)
    out_specs = pl.BlockSpec((tm, tn), lambda i, j: (i, j))
    out_shape = jax.ShapeDtypeStruct((n, nout), F32)
    if wa is not None:
        in_specs.append(pl.BlockSpec(wa.shape, lambda i, j: (0, 0)))
        args.append(wa)
        out_specs = [out_specs, pl.BlockSpec((tm, wa.shape[1]), lambda i, j: (i, 0))]
        out_shape = [out_shape, jax.ShapeDtypeStruct((n, wa.shape[1]), F32)]
    return pl.pallas_call(
        functools.partial(_nmm_kernel, glu=glu, bias=b is not None, aux=wa is not None),
        grid=(n // tm, nj),
        in_specs=in_specs,
        out_specs=out_specs,
        out_shape=out_shape,
        scratch_shapes=[pltpu.VMEM((tm, d), BF16)],
        compiler_params=_params("parallel", "arbitrary"),
        name="norm_mod_matmul",
    )(*args)


def _proj_res_kernel(*refs, bias):
    a_ref, w_ref = refs[:2]
    pos = 2
    b_ref = None
    if bias:
        b_ref = refs[pos]; pos += 1
    x_ref, gt_ref, o_ref = refs[pos:pos + 3]
    a = a_ref[...]
    tn = min(PROJ_TN, o_ref.shape[1])
    for c0 in range(0, o_ref.shape[1], tn):
        cols = slice(c0, c0 + tn)
        y = jnp.dot(a, w_ref[:, cols], preferred_element_type=F32)
        if bias:
            y = y + b_ref[:, cols]
        o_ref[:, cols] = x_ref[:, cols] + gt_ref[:, cols] * y


def _proj_residual(a, w, layer, b, x, mods, kg, rows_per_batch):
    n, kdim = a.shape
    d = w.shape[2]
    tm = min(PROJ_TM, rows_per_batch)
    in_specs = [pl.BlockSpec((tm, kdim), lambda i: (i, 0)),
                pl.BlockSpec((None, kdim, d), lambda i: (layer, 0, 0), pipeline_mode=pl.Buffered(1))]
    args = [a, w]
    if b is not None:
        in_specs.append(pl.BlockSpec((1, d), lambda i: (0, 0)))
        args.append(b)
    in_specs += [pl.BlockSpec((tm, d), lambda i: (i, 0)),
                 pl.BlockSpec((None, None, 1, d), lambda i: (i * tm // rows_per_batch, kg, 0, 0))]
    args += [x, mods]
    return pl.pallas_call(
        functools.partial(_proj_res_kernel, bias=b is not None),
        grid=(n // tm,),
        in_specs=in_specs,
        out_specs=pl.BlockSpec((tm, d), lambda i: (i, 0)),
        out_shape=jax.ShapeDtypeStruct((n, d), F32),
        compiler_params=_params("parallel"),
        name="proj_residual",
    )(*args)


def _dwconv_kernel(prev_ref, main_ref, next_ref, w_ref, bdw_ref, lg_ref, lb_ref, o_ref,
                   buf_ref, conv_ref):
    i = pl.program_id(1)
    tt, d = main_ref.shape
    halo = prev_ref.shape[0]
    pad = CONV_WIDTH // 2
    buf_ref[0:halo, :] = jnp.where(i > 0, prev_ref[...], 0.0)
    buf_ref[halo:halo + tt, :] = main_ref[...]
    buf_ref[halo + tt:, :] = jnp.where(i < pl.num_programs(1) - 1, next_ref[...], 0.0)

    rc, lc = CONV_ROWS, LANES
    win_rows = rc + 2 * halo
    for r0 in range(0, tt, rc):
        for c0 in range(0, d, lc):
            win = buf_ref[r0:r0 + win_rows, c0:c0 + lc]
            acc = jnp.zeros((rc, lc), F32)
            for res in range(SUBLANES):
                shifted = win if res == 0 else pltpu.roll(win, win_rows - res, axis=0)
                for base in range(0, 2 * halo, SUBLANES):
                    k = base + res - (halo - pad)
                    if 0 <= k < CONV_WIDTH:
                        acc = acc + shifted[base:base + rc] * w_ref[k:k + 1, c0:c0 + lc]
            conv_ref[r0:r0 + rc, c0:c0 + lc] = acc + bdw_ref[:, c0:c0 + lc]

    u = conv_ref[...]
    mu = jnp.mean(u, axis=-1, keepdims=True)
    uc = u - mu
    var = jnp.mean(uc * uc, axis=-1, keepdims=True)
    y = uc * lax.rsqrt(var + EPS) * lg_ref[...] + lb_ref[...]
    o_ref[...] = (y * _sigmoid(y)).astype(o_ref.dtype)


def _dwconv_ln_silu(u, w_dw, b_dw, ln_g, ln_b, batch):
    n, d = u.shape
    t = n // batch
    tt, halo = CONV_TT, CONV_HALO
    nt = t // tt
    hb = tt // halo
    kw = w_dw.shape[0]
    kwp = -(-kw // SUBLANES) * SUBLANES
    wp = jnp.zeros((kwp, d), F32).at[:kw].set(w_dw)
    vec = pl.BlockSpec((1, d), lambda b, i: (0, 0))
    return pl.pallas_call(
        _dwconv_kernel,
        grid=(batch, nt),
        in_specs=[
            pl.BlockSpec((halo, d), lambda b, i: (jnp.maximum((b * nt + i) * hb - 1, 0), 0)),
            pl.BlockSpec((tt, d), lambda b, i: (b * nt + i, 0)),
            pl.BlockSpec((halo, d), lambda b, i: (jnp.minimum((b * nt + i + 1) * hb, n // halo - 1), 0)),
            pl.BlockSpec((kwp, d), lambda b, i: (0, 0)),
            vec, vec, vec,
        ],
        out_specs=pl.BlockSpec((tt, d), lambda b, i: (b * nt + i, 0)),
        out_shape=jax.ShapeDtypeStruct((n, d), BF16),
        scratch_shapes=[pltpu.VMEM((tt + 2 * halo, d), F32), pltpu.VMEM((tt, d), F32)],
        compiler_params=_params("parallel", "arbitrary"),
        name="dwconv_ln_silu",
    )(u, u, u, wp, b_dw, ln_g, ln_b)


def _pool_kernel(prev_ref, x_ref, next_ref, ng_ref, sh_ref, sc_ref, gt_ref, w_ref, b_ref, s_ref,
                 o_ref, hbuf_ref, *, seq_len):
    i = pl.program_id(1)
    tt, d = x_ref.shape
    halo = prev_ref.shape[0]
    ng, sh, sc = ng_ref[...], sh_ref[...], sc_ref[...]
    hbuf_ref[0:halo, :] = jnp.where(i > 0, _modnorm(prev_ref[...], ng, sh, sc), 0.0)
    hbuf_ref[halo:halo + tt, :] = _modnorm(x_ref[...], ng, sh, sc)
    hbuf_ref[halo + tt:, :] = jnp.where(i < pl.num_programs(1) - 1,
                                        _modnorm(next_ref[...], ng, sh, sc), 0.0)
    t = i * tt + lax.broadcasted_iota(jnp.int32, (tt, 1), 0)
    grp = d // len(POOL_WINDOWS)
    for gi, w in enumerate(POOL_WINDOWS):
        cs = slice(gi * grp, (gi + 1) * grp)
        left, right = w // 2, w - w // 2 - 1
        assert w == 2 * left and left & (left - 1) == 0 and left <= halo
        rows_all = tt + 2 * halo
        run = hbuf_ref[:, cs]
        span = 1
        while span < left:
            run = run + pltpu.roll(run, rows_all - span, axis=0)
            span *= 2
        if left % SUBLANES == 0:
            acc = run[halo - left:halo - left + tt] + run[halo:halo + tt]
        else:
            acc = (pltpu.roll(run, left, axis=0) + run)[halo:halo + tt]
        lo = jnp.maximum(t - left, 0)
        hi = jnp.minimum(t + right, seq_len - 1)
        cnt = (hi - lo + 1).astype(F32)
        dlt = acc / cnt - hbuf_ref[halo:halo + tt, cs]
        y = jnp.dot(dlt.astype(BF16), w_ref[gi], preferred_element_type=F32) + b_ref[:, cs]
        o_ref[:, cs] = x_ref[:, cs] + gt_ref[:, cs] * (y * s_ref[:, cs])


def _pool_sublayer(x, mods, ng, w, b, scale, batch):
    n, d = x.shape
    t = n // batch
    tt, halo = POOL_TT, POOL_HALO
    nt = t // tt
    hb = tt // halo
    vec = pl.BlockSpec((1, d), lambda b_, i: (0, 0))

    def mod(k):
        return pl.BlockSpec((None, None, 1, d), lambda b_, i: (b_, k, 0, 0))

    return pl.pallas_call(
        functools.partial(_pool_kernel, seq_len=t),
        grid=(batch, nt),
        in_specs=[
            pl.BlockSpec((halo, d), lambda b_, i: (jnp.maximum((b_ * nt + i) * hb - 1, 0), 0)),
            pl.BlockSpec((tt, d), lambda b_, i: (b_ * nt + i, 0)),
            pl.BlockSpec((halo, d), lambda b_, i: (jnp.minimum((b_ * nt + i + 1) * hb, n // halo - 1), 0)),
            vec, mod(3), mod(4), mod(5),
            pl.BlockSpec(w.shape, lambda b_, i: (0, 0, 0)),
            vec, vec,
        ],
        out_specs=pl.BlockSpec((tt, d), lambda b_, i: (b_ * nt + i, 0)),
        out_shape=jax.ShapeDtypeStruct((n, d), F32),
        scratch_shapes=[pltpu.VMEM((tt + 2 * halo, d), F32)],
        compiler_params=_params("parallel", "arbitrary"),
        name="pool_sublayer",
    )(x, x, x, ng, mods, mods, mods, w, b, scale)


def _gla_tables(c):
    nl = int(math.log2(c))
    idx = np.arange(c)
    t, r = idx[:, None], idx[None, :]
    stacks, masks = [], []
    for direction in (0, 1):
        fwd = direction == 0
        blocks = [
            (r <= t) if fwd else (r >= t),
            (r > t) if fwd else (r < t),
        ]
        msk = [np.eye(c, dtype=bool) if fwd else np.zeros((c, c), bool)]
        for lvl in range(1, nl + 1):
            n = 1 << lvl
            m = (idx // n) * n + n // 2
            mm = m[:, None]
            upper = (idx >= m)[:, None]
            if fwd:
                blk = np.where(upper, (r >= mm) & (r <= t), (r > t) & (r < mm))
                pair = upper & (idx < m)[None, :]
            else:
                blk = np.where(upper, (r >= mm) & (r < t), (r >= t) & (r < mm))
                pair = (~upper) & (idx >= m)[None, :]
            same = (idx // n)[:, None] == (idx // n)[None, :]
            blocks.append(blk)
            msk.append(pair & same)
        blocks.append(np.ones((SUBLANES, c), bool))
        stack = np.concatenate(blocks, axis=0)
        stacks.append(np.concatenate([stack, stack], axis=1))
        masks.append(np.stack(msk))
    return (np.stack(stacks).astype(np.float32), np.stack(masks).astype(np.float32))


def _gla_kernel(q_ref, k_ref, v_ref, og_ref, r_ref, wa2_ref, ba_ref, ng_ref, wst_ref, msk_ref,
                o_ref, oacc_ref, s_ref, vt_ref, qb_ref, kl_ref, dec_ref, fa_ref, fb_ref, *, chunk):
    t_len, hk = q_ref.shape
    c = chunk
    n_chunks = t_len // c
    n_lvl = msk_ref.shape[1] - 1
    qscale = hk ** -0.5
    nt_dims = (((1,), (1,)), ((), ()))

    def chunk_rows(ci):
        return pl.ds(ci * c if isinstance(ci, int) else pl.multiple_of(ci * c, c), c)

    dirs = (0, 1)

    def gate_logits(ci):
        rr = r_ref[chunk_rows(ci), :].astype(BF16)
        return [jnp.dot(rr, wa2_ref[dr], preferred_element_type=F32) + ba_ref[dr] for dr in dirs]

    def range_sums(z):
        xs = []
        for dr in dirs:
            g = (jnp.minimum(z[dr], 0.0) - jnp.log1p(jnp.exp(-jnp.abs(z[dr])))) * (1.0 / GLA_GATE_TEMP)
            g_hi = g.astype(BF16)
            g_lo = (g - g_hi.astype(F32)).astype(BF16)
            xs.append(jnp.dot(wst_ref[dr], jnp.concatenate([g_hi, g_lo], axis=0),
                              preferred_element_type=F32))
        return xs

    def store_factors(xs, f_ref):
        for dr in dirs:
            f_ref[dr] = jnp.exp(xs[dr])

    def local(ci, f_ref, f_next_ref):
        z_next = gate_logits(jnp.minimum(ci + 1, n_chunks - 1))
        rows = chunk_rows(ci)
        q = q_ref[rows, :] * qscale
        k = k_ref[rows, :]
        v = v_ref[rows, :]
        vb = v.astype(BF16)
        vt_ref[ci] = jnp.transpose(v).astype(BF16)

        def blk(dr, b):
            return f_ref[dr, b * c:(b + 1) * c, :]

        scores = [msk_ref[0, 0] * lax.dot_general(q.astype(BF16), k.astype(BF16), nt_dims,
                                                  preferred_element_type=F32),
                  jnp.zeros((c, c), F32)]
        xs_next = None
        for lvl in range(1, n_lvl + 1):
            for dr in dirs:
                fl = blk(dr, 1 + lvl)
                scores[dr] = scores[dr] + msk_ref[dr, lvl] * lax.dot_general(
                    (q * fl).astype(BF16), (k * fl).astype(BF16), nt_dims,
                    preferred_element_type=F32)
            if lvl == min(GLA_SUMS_AFTER, n_lvl):
                xs_next = range_sums(z_next)
        store_factors(xs_next, f_next_ref)
        oacc_ref[rows, :] = jnp.dot((scores[0] + scores[1]).astype(BF16), vb,
                                    preferred_element_type=F32)
        for dr in dirs:
            qb_ref[dr, rows, :] = (q * blk(dr, 0)).astype(BF16)
            kl_ref[dr, rows, :] = (k * blk(dr, 1)).astype(BF16)
            dec_ref[dr, ci] = f_ref[dr, (2 + n_lvl) * c:, :]

    assert n_chunks % 2 == 0

    def local_pair(p, carry):
        local(2 * p, fa_ref, fb_ref)
        local(2 * p + 1, fb_ref, fa_ref)
        return carry

    store_factors(range_sums(gate_logits(0)), fa_ref)
    lax.fori_loop(0, n_chunks // 2, local_pair, 0)

    s_ref[...] = jnp.zeros_like(s_ref)

    def finish(ci):
        rows = chunk_rows(ci)
        o = oacc_ref[rows, :]
        ms = jnp.mean(o * o, axis=-1, keepdims=True)
        y = o * lax.rsqrt(ms + EPS) * ng_ref[...]
        og = og_ref[rows, :]
        o_ref[rows, :] = (y * (og * _sigmoid(og))).astype(o_ref.dtype)

    def carried(it, carry, *, complete):
        for direction in (0, 1):
            ci = it if direction == 0 else n_chunks - 1 - it
            rows = chunk_rows(ci)
            state = s_ref[direction]
            oacc_ref[rows, :] += lax.dot_general(qb_ref[direction, rows, :], state.astype(BF16),
                                                 nt_dims, preferred_element_type=F32)
            s_ref[direction] = (state * dec_ref[direction, ci][0:1]
                                + jnp.dot(vt_ref[ci], kl_ref[direction, rows, :],
                                          preferred_element_type=F32))
        if complete:
            finish(it)
            finish(n_chunks - 1 - it)
        return carry

    half = n_chunks // 2
    lax.fori_loop(0, half, functools.partial(carried, complete=False), 0, unroll=4)
    lax.fori_loop(half, n_chunks, functools.partial(carried, complete=True), 0, unroll=4)


def _gla_scan(proj, r, wa2p, ba, norm_g, batch, hk, hv):
    n = proj.shape[0]
    t = n // batch
    c = GLA_CHUNK
    heads = GLA_HEADS
    wst, msk = _gla_tables(c)
    wst = jnp.asarray(wst, BF16)
    msk = jnp.asarray(msk, F32)
    kq, kk = 0, heads
    kv, kg = (2 * heads * hk) // hv, (2 * heads * hk) // hv + heads
    return pl.pallas_call(
        functools.partial(_gla_kernel, chunk=c),
        grid=(batch, heads),
        in_specs=[
            pl.BlockSpec((t, hk), lambda b, h: (b, kq + h)),
            pl.BlockSpec((t, hk), lambda b, h: (b, kk + h)),
            pl.BlockSpec((t, hv), lambda b, h: (b, kv + h)),
            pl.BlockSpec((t, hv), lambda b, h: (b, kg + h)),
            pl.BlockSpec((t, r.shape[1]), lambda b, h: (b, 0)),
            pl.BlockSpec((2, wa2p.shape[1], hk), lambda b, h: (0, 0, h)),
            pl.BlockSpec((2, 1, hk), lambda b, h: (0, 0, h)),
            pl.BlockSpec((1, hv), lambda b, h: (0, 0)),
            pl.BlockSpec(wst.shape, lambda b, h: (0, 0, 0)),
            pl.BlockSpec(msk.shape, lambda b, h: (0, 0, 0, 0)),
        ],
        out_specs=pl.BlockSpec((t, hv), lambda b, h: (b, h)),
        out_shape=jax.ShapeDtypeStruct((n, heads * hv), BF16),
        scratch_shapes=[
            pltpu.VMEM((t, hv), F32),
            pltpu.VMEM((2, hv, hk), F32),
            pltpu.VMEM((t // c, hv, c), BF16),
            pltpu.VMEM((2, t, hk), BF16),
            pltpu.VMEM((2, t, hk), BF16),
            pltpu.VMEM((2, t // c, SUBLANES, hk), F32),
            pltpu.VMEM((2, wst.shape[1], hk), F32),
            pltpu.VMEM((2, wst.shape[1], hk), F32),
        ],
        compiler_params=_params("parallel", "arbitrary"),
        name="gla_scan",
    )(proj, proj, proj, proj, r, wa2p, ba, norm_g, wst, msk)


def kernel(x, c, ada_w, ada_b, norm_g, ffn_w_gate, ffn_w_up, ffn_w_down, conv_w_in, conv_b_in,
           conv_w_dw, conv_b_dw, conv_ln_g, conv_ln_b, conv_w_out, conv_b_out, gla_w_in, gla_wa1,
           gla_wa2, gla_ba, gla_norm_g, gla_w_out, pool_w, pool_b, pool_scale, final_g):
    batch, t, d = x.shape
    depth = ada_w.shape[0]
    n = batch * t
    n_mixers = 3

    mods_all = _ada_mods(c, ada_w, ada_b).reshape(depth, batch, N_MOD, 1, d)
    fg = final_g.reshape(1, d)
    conv_w_in_bf, conv_w_out_bf = conv_w_in.astype(BF16), conv_w_out.astype(BF16)
    gla_w_in_bf, gla_w_out_bf = gla_w_in.astype(BF16), gla_w_out.astype(BF16)

    xs = x.reshape(n, d)
    for i in range(depth):
        mods = mods_all[i]
        ng = norm_g[i].reshape(3, 1, d)
        xs = _ffn(xs, mods, 0, ng[0], ffn_w_gate, ffn_w_up, ffn_w_down, i, 0, fg, t, final=False)

        kind, j = i % n_mixers, i // n_mixers
        if kind == 0:
            u = _norm_mod_matmul(xs, mods, 3, ng[1], conv_w_in_bf, j,
                                 conv_b_in[j].reshape(1, 2 * d), t, glu=True)
            a = _dwconv_ln_silu(u, conv_w_dw[j], conv_b_dw[j].reshape(1, d),
                                conv_ln_g[j].reshape(1, d), conv_ln_b[j].reshape(1, d), batch)
            xs = _proj_residual(a, conv_w_out_bf, j, conv_b_out[j].reshape(1, d),
                                xs, mods, 5, t)
        elif kind == 1:
            dk = gla_wa2.shape[-1]
            rank = gla_wa1.shape[-1]
            hk, hv = dk // GLA_HEADS, d // GLA_HEADS
            wa1p = jnp.zeros((d, LANES), F32)
            wa2p = jnp.zeros((2, LANES, dk), F32)
            for dr in (0, 1):
                wa1p = wa1p.at[:, dr * rank:(dr + 1) * rank].set(gla_wa1[j, dr])
                wa2p = wa2p.at[dr, dr * rank:(dr + 1) * rank].set(gla_wa2[j, dr])
            proj, r = _norm_mod_matmul(xs, mods, 3, ng[1], gla_w_in_bf, j, None, t,
                                       glu=False, wa=wa1p.astype(BF16))
            y = _gla_scan(proj, r, wa2p.astype(BF16), gla_ba[j].reshape(2, 1, dk),
                          gla_norm_g[j].reshape(1, hv), batch, hk, hv)
            xs = _proj_residual(y, gla_w_out_bf, j, None, xs, mods, 5, t)
        else:
            xs = _pool_sublayer(xs, mods, ng[1], pool_w[j].astype(BF16), pool_b[j].reshape(1, d),
                                pool_scale[j].reshape(1, d), batch)

        xs = _ffn(xs, mods, 6, ng[2], ffn_w_gate, ffn_w_up, ffn_w_down, i, 1, fg, t,
                  final=(i == depth - 1))
    return xs.reshape(batch, t, d)
```

```python
import functools
import math

import numpy as np
import jax
import jax.numpy as jnp
from jax import lax
from jax.experimental import pallas as pl
from jax.experimental.pallas import tpu as pltpu

F32 = jnp.float32
BF16 = jnp.bfloat16

EPS = 1e-6
N_MOD = 9
CONV_WIDTH = 31
GLA_HEADS = 4
GLA_GATE_TEMP = 16.0
POOL_WINDOWS = (2, 4, 8, 16)

LANES = 128
SUBLANES = 8
BF16_SUBLANES = 16
VMEM_LIMIT = 56 * 1024 * 1024

ROW_CHUNK = 128
ADA_TN = 1024
GLA_CHUNK = 128
GLA_SUMS_AFTER = 3
FFN_TM = 1024
FFN_TF = 256
FFN_TN = 512
FFN_ROWS = 256
MM_TM = 1024
MM_TN = 1024
PROJ_TM = 1024
PROJ_TN = 512
MM_ROWS = 256
CONV_TT = 128
CONV_HALO = 16
CONV_ROWS = 64
POOL_TT = 512
POOL_HALO = 8


def _params(*sem):
    return pltpu.CompilerParams(dimension_semantics=sem, vmem_limit_bytes=VMEM_LIMIT)


def _sigmoid(x):
    return 1.0 / (1.0 + jnp.exp(-x))


def _modnorm(x, ng, sh, sc):
    ms = jnp.mean(x * x, axis=-1, keepdims=True)
    return (x * lax.rsqrt(ms + EPS)) * (ng * (1.0 + sc)) + sh


def _modnorm_rows(x_ref, h_ref, ng_ref, sh_ref, sc_ref):
    ng, sh, sc = ng_ref[...], sh_ref[...], sc_ref[...]

    def body(r, carry):
        rows = pl.ds(pl.multiple_of(r * ROW_CHUNK, ROW_CHUNK), ROW_CHUNK)
        h_ref[rows, :] = _modnorm(x_ref[rows, :], ng, sh, sc).astype(BF16)
        return carry

    lax.fori_loop(0, x_ref.shape[0] // ROW_CHUNK, body, 0)


def _ada_kernel(c_ref, w_ref, b_ref, o_ref):
    c = c_ref[...]
    ca = (c * _sigmoid(c)).astype(BF16)
    o_ref[0] = jnp.dot(ca, w_ref[0].astype(BF16), preferred_element_type=F32) + b_ref[0]


def _ada_mods(c, ada_w, ada_b):
    depth, d, nd = ada_w.shape
    b = c.shape[0]
    bp = -(-b // BF16_SUBLANES) * BF16_SUBLANES
    cp = jnp.zeros((bp, d), F32).at[:b].set(c)
    tn = min(ADA_TN, nd)
    out = pl.pallas_call(
        _ada_kernel,
        grid=(depth, nd // tn),
        in_specs=[
            pl.BlockSpec((bp, d), lambda l, j: (0, 0)),
            pl.BlockSpec((1, d, tn), lambda l, j: (l, 0, j)),
            pl.BlockSpec((1, 1, tn), lambda l, j: (l, 0, j)),
        ],
        out_specs=pl.BlockSpec((1, bp, tn), lambda l, j: (l, 0, j)),
        out_shape=jax.ShapeDtypeStruct((depth, bp, nd), F32),
        compiler_params=_params("parallel", "parallel"),
        name="ada_mods",
    )(cp, ada_w, ada_b.reshape(depth, 1, nd))
    return out[:, :b]


def _ffn_kernel(x_ref, ng_ref, sh_ref, sc_ref, gt_ref, fg_ref, wg_ref, wu_ref, wd_ref,
                wgr_ref, wur_ref, wdr_ref, o_ref, h_ref, *, final, rem):
    j = pl.program_id(1)
    d = x_ref.shape[1]
    half_gate = 0.5 * gt_ref[...]

    def swiglu(g, u):
        return (g * _sigmoid(g) * u).astype(BF16)

    @pl.when(j == 0)
    def _():
        if rem:
            ng, sh, sc = ng_ref[...], sh_ref[...], sc_ref[...]
            wgu = jnp.concatenate([wgr_ref[...].astype(BF16), wur_ref[...].astype(BF16)], axis=1)
            wdr = wdr_ref[...].astype(BF16)
            for r0 in range(0, x_ref.shape[0], 2 * FFN_ROWS):
                pair = [slice(r, r + FFN_ROWS) for r in (r0, r0 + FFN_ROWS)]
                gus = []
                for rows in pair:
                    hb = _modnorm(x_ref[rows, :], ng, sh, sc).astype(BF16)
                    h_ref[rows, :] = hb
                    gus.append(jnp.dot(hb, wgu, preferred_element_type=F32))
                for rows, gu in zip(pair, gus):
                    a = swiglu(gu[:, :rem], gu[:, rem:])
                    o_ref[rows, :] = x_ref[rows, :] + half_gate * jnp.dot(
                        a, wdr, preferred_element_type=F32)
        else:
            _modnorm_rows(x_ref, h_ref, ng_ref, sh_ref, sc_ref)
            o_ref[...] = x_ref[...]

    h = h_ref[...]
    g = jnp.dot(h, wg_ref[...].astype(BF16), preferred_element_type=F32)
    u = jnp.dot(h, wu_ref[...].astype(BF16), preferred_element_type=F32)
    a = swiglu(g, u)
    for c0 in range(0, d, FFN_TN):
        cols = slice(c0, c0 + FFN_TN)
        o_ref[:, cols] += half_gate[:, cols] * jnp.dot(a, wd_ref[:, cols].astype(BF16),
                                                       preferred_element_type=F32)

    if final:
        @pl.when(j == pl.num_programs(1) - 1)
        def _():
            fg = fg_ref[...]

            def body(r, carry):
                rows = pl.ds(pl.multiple_of(r * ROW_CHUNK, ROW_CHUNK), ROW_CHUNK)
                o = o_ref[rows, :]
                ms = jnp.mean(o * o, axis=-1, keepdims=True)
                o_ref[rows, :] = o * lax.rsqrt(ms + EPS) * fg
                return carry

            lax.fori_loop(0, o_ref.shape[0] // ROW_CHUNK, body, 0)


def _ffn(x, mods, k0, ng, wg, wu, wd, layer, slot, final_g, rows_per_batch, final):
    n, d = x.shape
    f = wg.shape[-1]
    tm, tf = min(FFN_TM, rows_per_batch), FFN_TF
    nj = f // tf
    rem = f - nj * tf
    rb = rem if rem else tf
    assert f % rb == 0 and rb % LANES == 0 and tm % (2 * FFN_ROWS) == 0
    last = f // rb - 1
    nt = n // tm
    vec = pl.BlockSpec((1, d), lambda i, j: (0, 0))
    once = pl.Buffered(1)

    def mod(k):
        return pl.BlockSpec((None, None, 1, d), lambda i, j: (i * tm // rows_per_batch, k, 0, 0))

    return pl.pallas_call(
        functools.partial(_ffn_kernel, final=final, rem=rem),
        grid=(n // tm, nj),
        in_specs=[
            pl.BlockSpec((tm, d), lambda i, j: (jnp.minimum(i + jnp.minimum(j, 1), nt - 1), 0)),
            vec, mod(k0), mod(k0 + 1), mod(k0 + 2), vec,
            pl.BlockSpec((None, None, d, tf), lambda i, j: (layer, slot, 0, j)),
            pl.BlockSpec((None, None, d, tf), lambda i, j: (layer, slot, 0, j)),
            pl.BlockSpec((None, None, tf, d), lambda i, j: (layer, slot, j, 0)),
            pl.BlockSpec((None, None, d, rb), lambda i, j: (layer, slot, 0, last), pipeline_mode=once),
            pl.BlockSpec((None, None, d, rb), lambda i, j: (layer, slot, 0, last), pipeline_mode=once),
            pl.BlockSpec((None, None, rb, d), lambda i, j: (layer, slot, last, 0), pipeline_mode=once),
        ],
        out_specs=pl.BlockSpec((tm, d), lambda i, j: (i, 0)),
        out_shape=jax.ShapeDtypeStruct((n, d), F32),
        scratch_shapes=[pltpu.VMEM((tm, d), BF16)],
        compiler_params=_params("parallel", "arbitrary"),
        name="ffn",
    )(x, ng, mods, mods, mods, final_g, wg, wu, wd, wg, wu, wd)


def _nmm_kernel(*refs, glu, bias, aux):
    x_ref, ng_ref, sh_ref, sc_ref = refs[:4]
    pos = 4
    w_ref = refs[pos]; pos += 1
    w2_ref = b_ref = b2_ref = wa_ref = None
    if glu:
        w2_ref = refs[pos]; pos += 1
    if bias:
        b_ref = refs[pos]; pos += 1
        if glu:
            b2_ref = refs[pos]; pos += 1
    if aux:
        wa_ref = refs[pos]; pos += 1
    o_ref = refs[pos]; pos += 1
    r_ref = None
    if aux:
        r_ref = refs[pos]; pos += 1
    h_ref = refs[pos]

    def project(h):
        y = jnp.dot(h, w_ref[...], preferred_element_type=F32)
        if bias:
            y = y + b_ref[...]
        if glu:
            y2 = jnp.dot(h, w2_ref[...], preferred_element_type=F32)
            if bias:
                y2 = y2 + b2_ref[...]
            y = y * _sigmoid(y2)
        return y.astype(o_ref.dtype)

    @pl.when(pl.program_id(1) == 0)
    def _():
        ng, sh, sc = ng_ref[...], sh_ref[...], sc_ref[...]
        for r0 in range(0, x_ref.shape[0], MM_ROWS):
            rows = slice(r0, r0 + MM_ROWS)
            hb = _modnorm(x_ref[rows, :], ng, sh, sc).astype(BF16)
            h_ref[rows, :] = hb
            if aux:
                r_ref[rows, :] = jnp.dot(hb, wa_ref[...], preferred_element_type=F32)
            o_ref[rows, :] = project(hb)

    @pl.when(pl.program_id(1) > 0)
    def _():
        o_ref[...] = project(h_ref[...])


def _norm_mod_matmul(x, mods, k0, ng, w, layer, b, rows_per_batch, *, glu, wa=None):
    n, d = x.shape
    nout = w.shape[2] // 2 if glu else w.shape[2]
    tm, tn = min(MM_TM, rows_per_batch), min(MM_TN, nout)
    while nout % tn:
        tn //= 2
    assert tn % LANES == 0
    nj = nout // tn
    vec = pl.BlockSpec((1, d), lambda i, j: (0, 0))

    def mod(k):
        return pl.BlockSpec((None, None, 1, d), lambda i, j: (i * tm // rows_per_batch, k, 0, 0))

    w_mode = pl.Buffered(1) if nj == 1 else None
    in_specs = [pl.BlockSpec((tm, d), lambda i, j: (i, 0)), vec, mod(k0), mod(k0 + 1),
                pl.BlockSpec((None, d, tn), lambda i, j: (layer, 0, j), pipeline_mode=w_mode)]
    args = [x, ng, mods, mods, w]
    if glu:
        in_specs.append(pl.BlockSpec((None, d, tn), lambda i, j: (layer, 0, j + nj),
                                     pipeline_mode=w_mode))
        args.append(w)
    if b is not None:
        in_specs.append(pl.BlockSpec((1, tn), lambda i, j: (0, j)))
        args.append(b)
        if glu:
            in_specs.append(pl.BlockSpec((1, tn), lambda i, j: (0, j + nj)))
            args.append(b)
    out_specs = pl.BlockSpec((tm, tn), lambda i, j: (i, j))
    out_shape = jax.ShapeDtypeStruct((n, nout), F32)
    if wa is not None:
        in_specs.append(pl.BlockSpec(wa.shape, lambda i, j: (0, 0)))
        args.append(wa)
        out_specs = [out_specs, pl.BlockSpec((tm, wa.shape[1]), lambda i, j: (i, 0))]
        out_shape = [out_shape, jax.ShapeDtypeStruct((n, wa.shape[1]), F32)]
    return pl.pallas_call(
        functools.partial(_nmm_kernel, glu=glu, bias=b is not None, aux=wa is not None),
        grid=(n // tm, nj),
        in_specs=in_specs,
        out_specs=out_specs,
        out_shape=out_shape,
        scratch_shapes=[pltpu.VMEM((tm, d), BF16)],
        compiler_params=_params("parallel", "arbitrary"),
        name="norm_mod_matmul",
    )(*args)


def _proj_res_kernel(*refs, bias):
    a_ref, w_ref = refs[:2]
    pos = 2
    b_ref = None
    if bias:
        b_ref = refs[pos]; pos += 1
    x_ref, gt_ref, o_ref = refs[pos:pos + 3]
    a = a_ref[...]
    tn = min(PROJ_TN, o_ref.shape[1])
    for c0 in range(0, o_ref.shape[1], tn):
        cols = slice(c0, c0 + tn)
        y = jnp.dot(a, w_ref[:, cols], preferred_element_type=F32)
        if bias:
            y = y + b_ref[:, cols]
        o_ref[:, cols] = x_ref[:, cols] + gt_ref[:, cols] * y


def _proj_residual(a, w, layer, b, x, mods, kg, rows_per_batch):
    n, kdim = a.shape
    d = w.shape[2]
    tm = min(PROJ_TM, rows_per_batch)
    in_specs = [pl.BlockSpec((tm, kdim), lambda i: (i, 0)),
                pl.BlockSpec((None, kdim, d), lambda i: (layer, 0, 0), pipeline_mode=pl.Buffered(1))]
    args = [a, w]
    if b is not None:
        in_specs.append(pl.BlockSpec((1, d), lambda i: (0, 0)))
        args.append(b)
    in_specs += [pl.BlockSpec((tm, d), lambda i: (i, 0)),
                 pl.BlockSpec((None, None, 1, d), lambda i: (i * tm // rows_per_batch, kg, 0, 0))]
    args += [x, mods]
    return pl.pallas_call(
        functools.partial(_proj_res_kernel, bias=b is not None),
        grid=(n // tm,),
        in_specs=in_specs,
        out_specs=pl.BlockSpec((tm, d), lambda i: (i, 0)),
        out_shape=jax.ShapeDtypeStruct((n, d), F32),
        compiler_params=_params("parallel"),
        name="proj_residual",
    )(*args)


def _dwconv_kernel(prev_ref, main_ref, next_ref, w_ref, bdw_ref, lg_ref, lb_ref, o_ref,
                   buf_ref, conv_ref):
    i = pl.program_id(1)
    tt, d = main_ref.shape
    halo = prev_ref.shape[0]
    pad = CONV_WIDTH // 2
    buf_ref[0:halo, :] = jnp.where(i > 0, prev_ref[...], 0.0)
    buf_ref[halo:halo + tt, :] = main_ref[...]
    buf_ref[halo + tt:, :] = jnp.where(i < pl.num_programs(1) - 1, next_ref[...], 0.0)

    rc, lc = CONV_ROWS, LANES
    win_rows = rc + 2 * halo
    for r0 in range(0, tt, rc):
        for c0 in range(0, d, lc):
            win = buf_ref[r0:r0 + win_rows, c0:c0 + lc]
            acc = jnp.zeros((rc, lc), F32)
            for res in range(SUBLANES):
                shifted = win if res == 0 else pltpu.roll(win, win_rows - res, axis=0)
                for base in range(0, 2 * halo, SUBLANES):
                    k = base + res - (halo - pad)
                    if 0 <= k < CONV_WIDTH:
                        acc = acc + shifted[base:base + rc] * w_ref[k:k + 1, c0:c0 + lc]
            conv_ref[r0:r0 + rc, c0:c0 + lc] = acc + bdw_ref[:, c0:c0 + lc]

    u = conv_ref[...]
    mu = jnp.mean(u, axis=-1, keepdims=True)
    uc = u - mu
    var = jnp.mean(uc * uc, axis=-1, keepdims=True)
    y = uc * lax.rsqrt(var + EPS) * lg_ref[...] + lb_ref[...]
    o_ref[...] = (y * _sigmoid(y)).astype(o_ref.dtype)


def _dwconv_ln_silu(u, w_dw, b_dw, ln_g, ln_b, batch):
    n, d = u.shape
    t = n // batch
    tt, halo = CONV_TT, CONV_HALO
    nt = t // tt
    hb = tt // halo
    kw = w_dw.shape[0]
    kwp = -(-kw // SUBLANES) * SUBLANES
    wp = jnp.zeros((kwp, d), F32).at[:kw].set(w_dw)
    vec = pl.BlockSpec((1, d), lambda b, i: (0, 0))
    return pl.pallas_call(
        _dwconv_kernel,
        grid=(batch, nt),
        in_specs=[
            pl.BlockSpec((halo, d), lambda b, i: (jnp.maximum((b * nt + i) * hb - 1, 0), 0)),
            pl.BlockSpec((tt, d), lambda b, i: (b * nt + i, 0)),
            pl.BlockSpec((halo, d), lambda b, i: (jnp.minimum((b * nt + i + 1) * hb, n // halo - 1), 0)),
            pl.BlockSpec((kwp, d), lambda b, i: (0, 0)),
            vec, vec, vec,
        ],
        out_specs=pl.BlockSpec((tt, d), lambda b, i: (b * nt + i, 0)),
        out_shape=jax.ShapeDtypeStruct((n, d), BF16),
        scratch_shapes=[pltpu.VMEM((tt + 2 * halo, d), F32), pltpu.VMEM((tt, d), F32)],
        compiler_params=_params("parallel", "arbitrary"),
        name="dwconv_ln_silu",
    )(u, u, u, wp, b_dw, ln_g, ln_b)


def _pool_kernel(prev_ref, x_ref, next_ref, ng_ref, sh_ref, sc_ref, gt_ref, w_ref, b_ref, s_ref,
                 o_ref, hbuf_ref, *, seq_len):
    i = pl.program_id(1)
    tt, d = x_ref.shape
    halo = prev_ref.shape[0]
    ng, sh, sc = ng_ref[...], sh_ref[...], sc_ref[...]
    hbuf_ref[0:halo, :] = jnp.where(i > 0, _modnorm(prev_ref[...], ng, sh, sc), 0.0)
    hbuf_ref[halo:halo + tt, :] = _modnorm(x_ref[...], ng, sh, sc)
    hbuf_ref[halo + tt:, :] = jnp.where(i < pl.num_programs(1) - 1,
                                        _modnorm(next_ref[...], ng, sh, sc), 0.0)
    t = i * tt + lax.broadcasted_iota(jnp.int32, (tt, 1), 0)
    grp = d // len(POOL_WINDOWS)
    for gi, w in enumerate(POOL_WINDOWS):
        cs = slice(gi * grp, (gi + 1) * grp)
        left, right = w // 2, w - w // 2 - 1
        assert w == 2 * left and left & (left - 1) == 0 and left <= halo
        rows_all = tt + 2 * halo
        run = hbuf_ref[:, cs]
        span = 1
        while span < left:
            run = run + pltpu.roll(run, rows_all - span, axis=0)
            span *= 2
        if left % SUBLANES == 0:
            acc = run[halo - left:halo - left + tt] + run[halo:halo + tt]
        else:
            acc = (pltpu.roll(run, left, axis=0) + run)[halo:halo + tt]
        lo = jnp.maximum(t - left, 0)
        hi = jnp.minimum(t + right, seq_len - 1)
        cnt = (hi - lo + 1).astype(F32)
        dlt = acc / cnt - hbuf_ref[halo:halo + tt, cs]
        y = jnp.dot(dlt.astype(BF16), w_ref[gi], preferred_element_type=F32) + b_ref[:, cs]
        o_ref[:, cs] = x_ref[:, cs] + gt_ref[:, cs] * (y * s_ref[:, cs])


def _pool_sublayer(x, mods, ng, w, b, scale, batch):
    n, d = x.shape
    t = n // batch
    tt, halo = POOL_TT, POOL_HALO
    nt = t // tt
    hb = tt // halo
    vec = pl.BlockSpec((1, d), lambda b_, i: (0, 0))

    def mod(k):
        return pl.BlockSpec((None, None, 1, d), lambda b_, i: (b_, k, 0, 0))

    return pl.pallas_call(
        functools.partial(_pool_kernel, seq_len=t),
        grid=(batch, nt),
        in_specs=[
            pl.BlockSpec((halo, d), lambda b_, i: (jnp.maximum((b_ * nt + i) * hb - 1, 0), 0)),
            pl.BlockSpec((tt, d), lambda b_, i: (b_ * nt + i, 0)),
            pl.BlockSpec((halo, d), lambda b_, i: (jnp.minimum((b_ * nt + i + 1) * hb, n // halo - 1), 0)),
            vec, mod(3), mod(4), mod(5),
            pl.BlockSpec(w.shape, lambda b_, i: (0, 0, 0)),
            vec, vec,
        ],
        out_specs=pl.BlockSpec((tt, d), lambda b_, i: (b_ * nt + i, 0)),
        out_shape=jax.ShapeDtypeStruct((n, d), F32),
        scratch_shapes=[pltpu.VMEM((tt + 2 * halo, d), F32)],
        compiler_params=_params("parallel", "arbitrary"),
        name="pool_sublayer",
    )(x, x, x, ng, mods, mods, mods, w, b, scale)


def _gla_tables(c):
    nl = int(math.log2(c))
    idx = np.arange(c)
    t, r = idx[:, None], idx[None, :]
    stacks, masks = [], []
    for direction in (0, 1):
        fwd = direction == 0
        blocks = [
            (r <= t) if fwd else (r >= t),
            (r > t) if fwd else (r < t),
        ]
        msk = [np.eye(c, dtype=bool) if fwd else np.zeros((c, c), bool)]
        for lvl in range(1, nl + 1):
            n = 1 << lvl
            m = (idx // n) * n + n // 2
            mm = m[:, None]
            upper = (idx >= m)[:, None]
            if fwd:
                blk = np.where(upper, (r >= mm) & (r <= t), (r > t) & (r < mm))
                pair = upper & (idx < m)[None, :]
            else:
                blk = np.where(upper, (r >= mm) & (r < t), (r >= t) & (r < mm))
                pair = (~upper) & (idx >= m)[None, :]
            same = (idx // n)[:, None] == (idx // n)[None, :]
            blocks.append(blk)
            msk.append(pair & same)
        blocks.append(np.ones((SUBLANES, c), bool))
        stack = np.concatenate(blocks, axis=0)
        stacks.append(np.concatenate([stack, stack], axis=1))
        masks.append(np.stack(msk))
    return (np.stack(stacks).astype(np.float32), np.stack(masks).astype(np.float32))


def _gla_kernel(q_ref, k_ref, v_ref, og_ref, r_ref, wa2_ref, ba_ref, ng_ref, wst_ref, msk_ref,
                o_ref, oacc_ref, s_ref, vt_ref, qb_ref, kl_ref, dec_ref, fa_ref, fb_ref, *, chunk):
    t_len, hk = q_ref.shape
    c = chunk
    n_chunks = t_len // c
    n_lvl = msk_ref.shape[1] - 1
    qscale = hk ** -0.5
    nt_dims = (((1,), (1,)), ((), ()))

    def chunk_rows(ci):
        return pl.ds(ci * c if isinstance(ci, int) else pl.multiple_of(ci * c, c), c)

    dirs = (0, 1)

    def gate_logits(ci):
        rr = r_ref[chunk_rows(ci), :].astype(BF16)
        return [jnp.dot(rr, wa2_ref[dr], preferred_element_type=F32) + ba_ref[dr] for dr in dirs]

    def range_sums(z):
        xs = []
        for dr in dirs:
            g = (jnp.minimum(z[dr], 0.0) - jnp.log1p(jnp.exp(-jnp.abs(z[dr])))) * (1.0 / GLA_GATE_TEMP)
            g_hi = g.astype(BF16)
            g_lo = (g - g_hi.astype(F32)).astype(BF16)
            xs.append(jnp.dot(wst_ref[dr], jnp.concatenate([g_hi, g_lo], axis=0),
                              preferred_element_type=F32))
        return xs

    def store_factors(xs, f_ref):
        for dr in dirs:
            f_ref[dr] = jnp.exp(xs[dr])

    def local(ci, f_ref, f_next_ref):
        z_next = gate_logits(jnp.minimum(ci + 1, n_chunks - 1))
        rows = chunk_rows(ci)
        q = q_ref[rows, :] * qscale
        k = k_ref[rows, :]
        v = v_ref[rows, :]
        vb = v.astype(BF16)
        vt_ref[ci] = jnp.transpose(v).astype(BF16)

        def blk(dr, b):
            return f_ref[dr, b * c:(b + 1) * c, :]

        scores = [msk_ref[0, 0] * lax.dot_general(q.astype(BF16), k.astype(BF16), nt_dims,
                                                  preferred_element_type=F32),
                  jnp.zeros((c, c), F32)]
        xs_next = None
        for lvl in range(1, n_lvl + 1):
            for dr in dirs:
                fl = blk(dr, 1 + lvl)
                scores[dr] = scores[dr] + msk_ref[dr, lvl] * lax.dot_general(
                    (q * fl).astype(BF16), (k * fl).astype(BF16), nt_dims,
                    preferred_element_type=F32)
            if lvl == min(GLA_SUMS_AFTER, n_lvl):
                xs_next = range_sums(z_next)
        store_factors(xs_next, f_next_ref)
        oacc_ref[rows, :] = jnp.dot((scores[0] + scores[1]).astype(BF16), vb,
                                    preferred_element_type=F32)
        for dr in dirs:
            qb_ref[dr, rows, :] = (q * blk(dr, 0)).astype(BF16)
            kl_ref[dr, rows, :] = (k * blk(dr, 1)).astype(BF16)
            dec_ref[dr, ci] = f_ref[dr, (2 + n_lvl) * c:, :]

    assert n_chunks % 2 == 0

    def local_pair(p, carry):
        local(2 * p, fa_ref, fb_ref)
        local(2 * p + 1, fb_ref, fa_ref)
        return carry

    store_factors(range_sums(gate_logits(0)), fa_ref)
    lax.fori_loop(0, n_chunks // 2, local_pair, 0)

    s_ref[...] = jnp.zeros_like(s_ref)

    def finish(ci):
        rows = chunk_rows(ci)
        o = oacc_ref[rows, :]
        ms = jnp.mean(o * o, axis=-1, keepdims=True)
        y = o * lax.rsqrt(ms + EPS) * ng_ref[...]
        og = og_ref[rows, :]
        o_ref[rows, :] = (y * (og * _sigmoid(og))).astype(o_ref.dtype)

    def carried(it, carry, *, complete):
        for direction in (0, 1):
            ci = it if direction == 0 else n_chunks - 1 - it
            rows = chunk_rows(ci)
            state = s_ref[direction]
            oacc_ref[rows, :] += lax.dot_general(qb_ref[direction, rows, :], state.astype(BF16),
                                                 nt_dims, preferred_element_type=F32)
            s_ref[direction] = (state * dec_ref[direction, ci][0:1]
                                + jnp.dot(vt_ref[ci], kl_ref[direction, rows, :],
                                          preferred_element_type=F32))
        if complete:
            finish(it)
            finish(n_chunks - 1 - it)
        return carry

    half = n_chunks // 2
    lax.fori_loop(0, half, functools.partial(carried, complete=False), 0, unroll=4)
    lax.fori_loop(half, n_chunks, functools.partial(carried, complete=True), 0, unroll=4)


def _gla_scan(proj, r, wa2p, ba, norm_g, batch, hk, hv):
    n = proj.shape[0]
    t = n // batch
    c = GLA_CHUNK
    heads = GLA_HEADS
    wst, msk = _gla_tables(c)
    wst = jnp.asarray(wst, BF16)
    msk = jnp.asarray(msk, F32)
    kq, kk = 0, heads
    kv, kg = (2 * heads * hk) // hv, (2 * heads * hk) // hv + heads
    return pl.pallas_call(
        functools.partial(_gla_kernel, chunk=c),
        grid=(batch, heads),
        in_specs=[
            pl.BlockSpec((t, hk), lambda b, h: (b, kq + h)),
            pl.BlockSpec((t, hk), lambda b, h: (b, kk + h)),
            pl.BlockSpec((t, hv), lambda b, h: (b, kv + h)),
            pl.BlockSpec((t, hv), lambda b, h: (b, kg + h)),
            pl.BlockSpec((t, r.shape[1]), lambda b, h: (b, 0)),
            pl.BlockSpec((2, wa2p.shape[1], hk), lambda b, h: (0, 0, h)),
            pl.BlockSpec((2, 1, hk), lambda b, h: (0, 0, h)),
            pl.BlockSpec((1, hv), lambda b, h: (0, 0)),
            pl.BlockSpec(wst.shape, lambda b, h: (0, 0, 0)),
            pl.BlockSpec(msk.shape, lambda b, h: (0, 0, 0, 0)),
        ],
        out_specs=pl.BlockSpec((t, hv), lambda b, h: (b, h)),
        out_shape=jax.ShapeDtypeStruct((n, heads * hv), BF16),
        scratch_shapes=[
            pltpu.VMEM((t, hv), F32),
            pltpu.VMEM((2, hv, hk), F32),
            pltpu.VMEM((t // c, hv, c), BF16),
            pltpu.VMEM((2, t, hk), BF16),
            pltpu.VMEM((2, t, hk), BF16),
            pltpu.VMEM((2, t // c, SUBLANES, hk), F32),
            pltpu.VMEM((2, wst.shape[1], hk), F32),
            pltpu.VMEM((2, wst.shape[1], hk), F32),
        ],
        compiler_params=_params("parallel", "arbitrary"),
        name="gla_scan",
    )(proj, proj, proj, proj, r, wa2p, ba, norm_g, wst, msk)


def kernel(x, c, ada_w, ada_b, norm_g, ffn_w_gate, ffn_w_up, ffn_w_down, conv_w_in, conv_b_in,
           conv_w_dw, conv_b_dw, conv_ln_g, conv_ln_b, conv_w_out, conv_b_out, gla_w_in, gla_wa1,
           gla_wa2, gla_ba, gla_norm_g, gla_w_out, pool_w, pool_b, pool_scale, final_g):
    batch, t, d = x.shape
    depth = ada_w.shape[0]
    n = batch * t
    n_mixers = 3

    mods_all = _ada_mods(c, ada_w, ada_b).reshape(depth, batch, N_MOD, 1, d)
    fg = final_g.reshape(1, d)
    conv_w_in_bf, conv_w_out_bf = conv_w_in.astype(BF16), conv_w_out.astype(BF16)
    gla_w_in_bf, gla_w_out_bf = gla_w_in.astype(BF16), gla_w_out.astype(BF16)

    xs = x.reshape(n, d)
    for i in range(depth):
        mods = mods_all[i]
        ng = norm_g[i].reshape(3, 1, d)
        xs = _ffn(xs, mods, 0, ng[0], ffn_w_gate, ffn_w_up, ffn_w_down, i, 0, fg, t, final=False)

        kind, j = i % n_mixers, i // n_mixers
        if kind == 0:
            u = _norm_mod_matmul(xs, mods, 3, ng[1], conv_w_in_bf, j,
                                 conv_b_in[j].reshape(1, 2 * d), t, glu=True)
            a = _dwconv_ln_silu(u, conv_w_dw[j], conv_b_dw[j].reshape(1, d),
                                conv_ln_g[j].reshape(1, d), conv_ln_b[j].reshape(1, d), batch)
            xs = _proj_residual(a, conv_w_out_bf, j, conv_b_out[j].reshape(1, d),
                                xs, mods, 5, t)
        elif kind == 1:
            dk = gla_wa2.shape[-1]
            rank = gla_wa1.shape[-1]
            hk, hv = dk // GLA_HEADS, d // GLA_HEADS
            wa1p = jnp.zeros((d, LANES), F32)
            wa2p = jnp.zeros((2, LANES, dk), F32)
            for dr in (0, 1):
                wa1p = wa1p.at[:, dr * rank:(dr + 1) * rank].set(gla_wa1[j, dr])
                wa2p = wa2p.at[dr, dr * rank:(dr + 1) * rank].set(gla_wa2[j, dr])
            proj, r = _norm_mod_matmul(xs, mods, 3, ng[1], gla_w_in_bf, j, None, t,
                                       glu=False, wa=wa1p.astype(BF16))
            y = _gla_scan(proj, r, wa2p.astype(BF16), gla_ba[j].reshape(2, 1, dk),
                          gla_norm_g[j].reshape(1, hv), batch, hk, hv)
            xs = _proj_residual(y, gla_w_out_bf, j, None, xs, mods, 5, t)
        else:
            xs = _pool_sublayer(xs, mods, ng[1], pool_w[j].astype(BF16), pool_b[j].reshape(1, d),
                                pool_scale[j].reshape(1, d), batch)

        xs = _ffn(xs, mods, 6, ng[2], ffn_w_gate, ffn_w_up, ffn_w_down, i, 1, fg, t,
                  final=(i == depth - 1))
    return xs.reshape(batch, t, d)
```

```python
import functools
import math

import numpy as np
import jax
import jax.numpy as jnp
from jax import lax
from jax.experimental import pallas as pl
from jax.experimental.pallas import tpu as pltpu

F32 = jnp.float32
BF16 = jnp.bfloat16

EPS = 1e-6
N_MOD = 9
CONV_WIDTH = 31
GLA_HEADS = 4
GLA_GATE_TEMP = 16.0
POOL_WINDOWS = (2, 4, 8, 16)

LANES = 128
SUBLANES = 8
BF16_SUBLANES = 16
VMEM_LIMIT = 56 * 1024 * 1024

ROW_CHUNK = 128
ADA_TN = 1024
GLA_CHUNK = 128
GLA_SUMS_AFTER = 3
FFN_TM = 1024
FFN_TF = 256
FFN_TN = 512
FFN_ROWS = 256
MM_TM = 1024
MM_TN = 1024
PROJ_TM = 512
PROJ_TN = 512
MM_ROWS = 256
CONV_TT = 128
CONV_HALO = 16
CONV_ROWS = 64
POOL_TT = 512
POOL_HALO = 8


def _params(*sem):
    return pltpu.CompilerParams(dimension_semantics=sem, vmem_limit_bytes=VMEM_LIMIT)


def _sigmoid(x):
    return 1.0 / (1.0 + jnp.exp(-x))


def _modnorm(x, ng, sh, sc):
    ms = jnp.mean(x * x, axis=-1, keepdims=True)
    return (x * lax.rsqrt(ms + EPS)) * (ng * (1.0 + sc)) + sh


def _modnorm_rows(x_ref, h_ref, ng_ref, sh_ref, sc_ref):
    ng, sh, sc = ng_ref[...], sh_ref[...], sc_ref[...]

    def body(r, carry):
        rows = pl.ds(pl.multiple_of(r * ROW_CHUNK, ROW_CHUNK), ROW_CHUNK)
        h_ref[rows, :] = _modnorm(x_ref[rows, :], ng, sh, sc).astype(BF16)
        return carry

    lax.fori_loop(0, x_ref.shape[0] // ROW_CHUNK, body, 0)


def _ada_kernel(c_ref, w_ref, b_ref, o_ref):
    c = c_ref[...]
    ca = (c * _sigmoid(c)).astype(BF16)
    o_ref[0] = jnp.dot(ca, w_ref[0].astype(BF16), preferred_element_type=F32) + b_ref[0]


def _ada_mods(c, ada_w, ada_b):
    depth, d, nd = ada_w.shape
    b = c.shape[0]
    bp = -(-b // BF16_SUBLANES) * BF16_SUBLANES
    cp = jnp.zeros((bp, d), F32).at[:b].set(c)
    tn = min(ADA_TN, nd)
    out = pl.pallas_call(
        _ada_kernel,
        grid=(depth, nd // tn),
        in_specs=[
            pl.BlockSpec((bp, d), lambda l, j: (0, 0)),
            pl.BlockSpec((1, d, tn), lambda l, j: (l, 0, j)),
            pl.BlockSpec((1, 1, tn), lambda l, j: (l, 0, j)),
        ],
        out_specs=pl.BlockSpec((1, bp, tn), lambda l, j: (l, 0, j)),
        out_shape=jax.ShapeDtypeStruct((depth, bp, nd), F32),
        compiler_params=_params("parallel", "parallel"),
        name="ada_mods",
    )(cp, ada_w, ada_b.reshape(depth, 1, nd))
    return out[:, :b]


def _ffn_kernel(x_ref, ng_ref, sh_ref, sc_ref, gt_ref, fg_ref, wg_ref, wu_ref, wd_ref,
                wgr_ref, wur_ref, wdr_ref, o_ref, h_ref, *, final, rem):
    j = pl.program_id(1)
    d = x_ref.shape[1]
    half_gate = 0.5 * gt_ref[...]

    def swiglu(g, u):
        return (g * _sigmoid(g) * u).astype(BF16)

    @pl.when(j == 0)
    def _():
        if rem:
            ng, sh, sc = ng_ref[...], sh_ref[...], sc_ref[...]
            wgu = jnp.concatenate([wgr_ref[...].astype(BF16), wur_ref[...].astype(BF16)], axis=1)
            wdr = wdr_ref[...].astype(BF16)
            for r0 in range(0, x_ref.shape[0], 2 * FFN_ROWS):
                pair = [slice(r, r + FFN_ROWS) for r in (r0, r0 + FFN_ROWS)]
                gus = []
                for rows in pair:
                    hb = _modnorm(x_ref[rows, :], ng, sh, sc).astype(BF16)
                    h_ref[rows, :] = hb
                    gus.append(jnp.dot(hb, wgu, preferred_element_type=F32))
                for rows, gu in zip(pair, gus):
                    a = swiglu(gu[:, :rem], gu[:, rem:])
                    o_ref[rows, :] = x_ref[rows, :] + half_gate * jnp.dot(
                        a, wdr, preferred_element_type=F32)
        else:
            _modnorm_rows(x_ref, h_ref, ng_ref, sh_ref, sc_ref)
            o_ref[...] = x_ref[...]

    h = h_ref[...]
    g = jnp.dot(h, wg_ref[...].astype(BF16), preferred_element_type=F32)
    u = jnp.dot(h, wu_ref[...].astype(BF16), preferred_element_type=F32)
    a = swiglu(g, u)
    for c0 in range(0, d, FFN_TN):
        cols = slice(c0, c0 + FFN_TN)
        o_ref[:, cols] += half_gate[:, cols] * jnp.dot(a, wd_ref[:, cols].astype(BF16),
                                                       preferred_element_type=F32)

    if final:
        @pl.when(j == pl.num_programs(1) - 1)
        def _():
            fg = fg_ref[...]

            def body(r, carry):
                rows = pl.ds(pl.multiple_of(r * ROW_CHUNK, ROW_CHUNK), ROW_CHUNK)
                o = o_ref[rows, :]
                ms = jnp.mean(o * o, axis=-1, keepdims=True)
                o_ref[rows, :] = o * lax.rsqrt(ms + EPS) * fg
                return carry

            lax.fori_loop(0, o_ref.shape[0] // ROW_CHUNK, body, 0)


def _ffn(x, mods, k0, ng, wg, wu, wd, layer, slot, final_g, rows_per_batch, final):
    n, d = x.shape
    f = wg.shape[-1]
    tm, tf = min(FFN_TM, rows_per_batch), FFN_TF
    nj = f // tf
    rem = f - nj * tf
    rb = rem if rem else tf
    assert f % rb == 0 and rb % LANES == 0 and tm % (2 * FFN_ROWS) == 0
    last = f // rb - 1
    nt = n // tm
    vec = pl.BlockSpec((1, d), lambda i, j: (0, 0))
    once = pl.Buffered(1)

    def mod(k):
        return pl.BlockSpec((None, None, 1, d), lambda i, j: (i * tm // rows_per_batch, k, 0, 0))

    return pl.pallas_call(
        functools.partial(_ffn_kernel, final=final, rem=rem),
        grid=(n // tm, nj),
        in_specs=[
            pl.BlockSpec((tm, d), lambda i, j: (jnp.minimum(i + jnp.minimum(j, 1), nt - 1), 0)),
            vec, mod(k0), mod(k0 + 1), mod(k0 + 2), vec,
            pl.BlockSpec((None, None, d, tf), lambda i, j: (layer, slot, 0, j)),
            pl.BlockSpec((None, None, d, tf), lambda i, j: (layer, slot, 0, j)),
            pl.BlockSpec((None, None, tf, d), lambda i, j: (layer, slot, j, 0)),
            pl.BlockSpec((None, None, d, rb), lambda i, j: (layer, slot, 0, last), pipeline_mode=once),
            pl.BlockSpec((None, None, d, rb), lambda i, j: (layer, slot, 0, last), pipeline_mode=once),
            pl.BlockSpec((None, None, rb, d), lambda i, j: (layer, slot, last, 0), pipeline_mode=once),
        ],
        out_specs=pl.BlockSpec((tm, d), lambda i, j: (i, 0)),
        out_shape=jax.ShapeDtypeStruct((n, d), F32),
        scratch_shapes=[pltpu.VMEM((tm, d), BF16)],
        compiler_params=_params("parallel", "arbitrary"),
        name="ffn",
    )(x, ng, mods, mods, mods, final_g, wg, wu, wd, wg, wu, wd)


def _nmm_kernel(*refs, glu, bias, aux):
    x_ref, ng_ref, sh_ref, sc_ref = refs[:4]
    pos = 4
    w_ref = refs[pos]; pos += 1
    w2_ref = b_ref = b2_ref = wa_ref = None
    if glu:
        w2_ref = refs[pos]; pos += 1
    if bias:
        b_ref = refs[pos]; pos += 1
        if glu:
            b2_ref = refs[pos]; pos += 1
    if aux:
        wa_ref = refs[pos]; pos += 1
    o_ref = refs[pos]; pos += 1
    r_ref = None
    if aux:
        r_ref = refs[pos]; pos += 1
    h_ref = refs[pos]

    def project(h):
        y = jnp.dot(h, w_ref[...], preferred_element_type=F32)
        if bias:
            y = y + b_ref[...]
        if glu:
            y2 = jnp.dot(h, w2_ref[...], preferred_element_type=F32)
            if bias:
                y2 = y2 + b2_ref[...]
            y = y * _sigmoid(y2)
        return y.astype(o_ref.dtype)

    @pl.when(pl.program_id(1) == 0)
    def _():
        ng, sh, sc = ng_ref[...], sh_ref[...], sc_ref[...]
        for r0 in range(0, x_ref.shape[0], MM_ROWS):
            rows = slice(r0, r0 + MM_ROWS)
            hb = _modnorm(x_ref[rows, :], ng, sh, sc).astype(BF16)
            h_ref[rows, :] = hb
            if aux:
                r_ref[rows, :] = jnp.dot(hb, wa_ref[...], preferred_element_type=F32)
            o_ref[rows, :] = project(hb)

    @pl.when(pl.program_id(1) > 0)
    def _():
        o_ref[...] = project(h_ref[...])


def _norm_mod_matmul(x, mods, k0, ng, w, layer, b, rows_per_batch, *, glu, wa=None):
    n, d = x.shape
    nout = w.shape[2] // 2 if glu else w.shape[2]
    tm, tn = min(MM_TM, rows_per_batch), min(MM_TN, nout)
    nj = nout // tn
    vec = pl.BlockSpec((1, d), lambda i, j: (0, 0))

    def mod(k):
        return pl.BlockSpec((None, None, 1, d), lambda i, j: (i * tm // rows_per_batch, k, 0, 0))

    in_specs = [pl.BlockSpec((tm, d), lambda i, j: (i, 0)), vec, mod(k0), mod(k0 + 1),
                pl.BlockSpec((None, d, tn), lambda i, j: (layer, 0, j))]
    args = [x, ng, mods, mods, w]
    if glu:
        in_specs.append(pl.BlockSpec((None, d, tn), lambda i, j: (layer, 0, j + nj)))
        args.append(w)
    if b is not None:
        in_specs.append(pl.BlockSpec((1, tn), lambda i, j: (0, j)))
        args.append(b)
        if glu:
            in_specs.append(pl.BlockSpec((1, tn), lambda i, j: (0, j + nj)))
            args.append(b)
    out_specs = pl.BlockSpec((tm, tn), lambda i, j: (i, j))
    out_shape = jax.ShapeDtypeStruct((n, nout), F32)
    if wa is not None:
        in_specs.append(pl.BlockSpec(wa.shape, lambda i, j: (0, 0)))
        args.append(wa)
        out_specs = [out_specs, pl.BlockSpec((tm, wa.shape[1]), lambda i, j: (i, 0))]
        out_shape = [out_shape, jax.ShapeDtypeStruct((n, wa.shape[1]), F32)]
    return pl.pallas_call(
        functools.partial(_nmm_kernel, glu=glu, bias=b is not None, aux=wa is not None),
        grid=(n // tm, nj),
        in_specs=in_specs,
        out_specs=out_specs,
        out_shape=out_shape,
        scratch_shapes=[pltpu.VMEM((tm, d), BF16)],
        compiler_params=_params("parallel", "arbitrary"),
        name="norm_mod_matmul",
    )(*args)


def _proj_res_kernel(*refs, bias):
    a_ref, w_ref = refs[:2]
    pos = 2
    b_ref = None
    if bias:
        b_ref = refs[pos]; pos += 1
    x_ref, gt_ref, o_ref = refs[pos:pos + 3]
    a = a_ref[...]
    tn = min(PROJ_TN, o_ref.shape[1])
    for c0 in range(0, o_ref.shape[1], tn):
        cols = slice(c0, c0 + tn)
        y = jnp.dot(a, w_ref[:, cols].astype(BF16), preferred_element_type=F32)
        if bias:
            y = y + b_ref[:, cols]
        o_ref[:, cols] = x_ref[:, cols] + gt_ref[:, cols] * y


def _proj_residual(a, w, layer, b, x, mods, kg, rows_per_batch):
    n, kdim = a.shape
    d = w.shape[2]
    tm = min(PROJ_TM, rows_per_batch)
    in_specs = [pl.BlockSpec((tm, kdim), lambda i: (i, 0)),
                pl.BlockSpec((None, kdim, d), lambda i: (layer, 0, 0), pipeline_mode=pl.Buffered(1))]
    args = [a, w]
    if b is not None:
        in_specs.append(pl.BlockSpec((1, d), lambda i: (0, 0)))
        args.append(b)
    in_specs += [pl.BlockSpec((tm, d), lambda i: (i, 0)),
                 pl.BlockSpec((None, None, 1, d), lambda i: (i * tm // rows_per_batch, kg, 0, 0))]
    args += [x, mods]
    return pl.pallas_call(
        functools.partial(_proj_res_kernel, bias=b is not None),
        grid=(n // tm,),
        in_specs=in_specs,
        out_specs=pl.BlockSpec((tm, d), lambda i: (i, 0)),
        out_shape=jax.ShapeDtypeStruct((n, d), F32),
        compiler_params=_params("parallel"),
        name="proj_residual",
    )(*args)


def _dwconv_kernel(prev_ref, main_ref, next_ref, w_ref, bdw_ref, lg_ref, lb_ref, o_ref,
                   buf_ref, conv_ref):
    i = pl.program_id(1)
    tt, d = main_ref.shape
    halo = prev_ref.shape[0]
    pad = CONV_WIDTH // 2
    buf_ref[0:halo, :] = jnp.where(i > 0, prev_ref[...], 0.0)
    buf_ref[halo:halo + tt, :] = main_ref[...]
    buf_ref[halo + tt:, :] = jnp.where(i < pl.num_programs(1) - 1, next_ref[...], 0.0)

    rc, lc = CONV_ROWS, LANES
    win_rows = rc + 2 * halo
    for r0 in range(0, tt, rc):
        for c0 in range(0, d, lc):
            win = buf_ref[r0:r0 + win_rows, c0:c0 + lc]
            acc = jnp.zeros((rc, lc), F32)
            for res in range(SUBLANES):
                shifted = win if res == 0 else pltpu.roll(win, win_rows - res, axis=0)
                for base in range(0, 2 * halo, SUBLANES):
                    k = base + res - (halo - pad)
                    if 0 <= k < CONV_WIDTH:
                        acc = acc + shifted[base:base + rc] * w_ref[k:k + 1, c0:c0 + lc]
            conv_ref[r0:r0 + rc, c0:c0 + lc] = acc + bdw_ref[:, c0:c0 + lc]

    u = conv_ref[...]
    mu = jnp.mean(u, axis=-1, keepdims=True)
    uc = u - mu
    var = jnp.mean(uc * uc, axis=-1, keepdims=True)
    y = uc * lax.rsqrt(var + EPS) * lg_ref[...] + lb_ref[...]
    o_ref[...] = (y * _sigmoid(y)).astype(o_ref.dtype)


def _dwconv_ln_silu(u, w_dw, b_dw, ln_g, ln_b, batch):
    n, d = u.shape
    t = n // batch
    tt, halo = CONV_TT, CONV_HALO
    nt = t // tt
    hb = tt // halo
    kw = w_dw.shape[0]
    kwp = -(-kw // SUBLANES) * SUBLANES
    wp = jnp.zeros((kwp, d), F32).at[:kw].set(w_dw)
    vec = pl.BlockSpec((1, d), lambda b, i: (0, 0))
    return pl.pallas_call(
        _dwconv_kernel,
        grid=(batch, nt),
        in_specs=[
            pl.BlockSpec((halo, d), lambda b, i: (jnp.maximum((b * nt + i) * hb - 1, 0), 0)),
            pl.BlockSpec((tt, d), lambda b, i: (b * nt + i, 0)),
            pl.BlockSpec((halo, d), lambda b, i: (jnp.minimum((b * nt + i + 1) * hb, n // halo - 1), 0)),
            pl.BlockSpec((kwp, d), lambda b, i: (0, 0)),
            vec, vec, vec,
        ],
        out_specs=pl.BlockSpec((tt, d), lambda b, i: (b * nt + i, 0)),
        out_shape=jax.ShapeDtypeStruct((n, d), BF16),
        scratch_shapes=[pltpu.VMEM((tt + 2 * halo, d), F32), pltpu.VMEM((tt, d), F32)],
        compiler_params=_params("parallel", "arbitrary"),
        name="dwconv_ln_silu",
    )(u, u, u, wp, b_dw, ln_g, ln_b)


def _pool_kernel(prev_ref, x_ref, next_ref, ng_ref, sh_ref, sc_ref, gt_ref, w_ref, b_ref, s_ref,
                 o_ref, hbuf_ref, *, seq_len):
    i = pl.program_id(1)
    tt, d = x_ref.shape
    halo = prev_ref.shape[0]
    ng, sh, sc = ng_ref[...], sh_ref[...], sc_ref[...]
    hbuf_ref[0:halo, :] = jnp.where(i > 0, _modnorm(prev_ref[...], ng, sh, sc), 0.0)
    hbuf_ref[halo:halo + tt, :] = _modnorm(x_ref[...], ng, sh, sc)
    hbuf_ref[halo + tt:, :] = jnp.where(i < pl.num_programs(1) - 1,
                                        _modnorm(next_ref[...], ng, sh, sc), 0.0)
    t = i * tt + lax.broadcasted_iota(jnp.int32, (tt, 1), 0)
    grp = d // len(POOL_WINDOWS)
    for gi, w in enumerate(POOL_WINDOWS):
        cs = slice(gi * grp, (gi + 1) * grp)
        left, right = w // 2, w - w // 2 - 1
        assert w == 2 * left and left & (left - 1) == 0 and left <= halo
        rows_all = tt + 2 * halo
        run = hbuf_ref[:, cs]
        span = 1
        while span < left:
            run = run + pltpu.roll(run, rows_all - span, axis=0)
            span *= 2
        if left % SUBLANES == 0:
            acc = run[halo - left:halo - left + tt] + run[halo:halo + tt]
        else:
            acc = (pltpu.roll(run, left, axis=0) + run)[halo:halo + tt]
        lo = jnp.maximum(t - left, 0)
        hi = jnp.minimum(t + right, seq_len - 1)
        cnt = (hi - lo + 1).astype(F32)
        dlt = acc / cnt - hbuf_ref[halo:halo + tt, cs]
        y = jnp.dot(dlt.astype(BF16), w_ref[gi], preferred_element_type=F32) + b_ref[:, cs]
        o_ref[:, cs] = x_ref[:, cs] + gt_ref[:, cs] * (y * s_ref[:, cs])


def _pool_sublayer(x, mods, ng, w, b, scale, batch):
    n, d = x.shape
    t = n // batch
    tt, halo = POOL_TT, POOL_HALO
    nt = t // tt
    hb = tt // halo
    vec = pl.BlockSpec((1, d), lambda b_, i: (0, 0))

    def mod(k):
        return pl.BlockSpec((None, None, 1, d), lambda b_, i: (b_, k, 0, 0))

    return pl.pallas_call(
        functools.partial(_pool_kernel, seq_len=t),
        grid=(batch, nt),
        in_specs=[
            pl.BlockSpec((halo, d), lambda b_, i: (jnp.maximum((b_ * nt + i) * hb - 1, 0), 0)),
            pl.BlockSpec((tt, d), lambda b_, i: (b_ * nt + i, 0)),
            pl.BlockSpec((halo, d), lambda b_, i: (jnp.minimum((b_ * nt + i + 1) * hb, n // halo - 1), 0)),
            vec, mod(3), mod(4), mod(5),
            pl.BlockSpec(w.shape, lambda b_, i: (0, 0, 0)),
            vec, vec,
        ],
        out_specs=pl.BlockSpec((tt, d), lambda b_, i: (b_ * nt + i, 0)),
        out_shape=jax.ShapeDtypeStruct((n, d), F32),
        scratch_shapes=[pltpu.VMEM((tt + 2 * halo, d), F32)],
        compiler_params=_params("parallel", "arbitrary"),
        name="pool_sublayer",
    )(x, x, x, ng, mods, mods, mods, w, b, scale)


def _gla_tables(c):
    nl = int(math.log2(c))
    idx = np.arange(c)
    t, r = idx[:, None], idx[None, :]
    stacks, masks = [], []
    for direction in (0, 1):
        fwd = direction == 0
        blocks = [
            (r <= t) if fwd else (r >= t),
            (r > t) if fwd else (r < t),
        ]
        msk = [np.eye(c, dtype=bool) if fwd else np.zeros((c, c), bool)]
        for lvl in range(1, nl + 1):
            n = 1 << lvl
            m = (idx // n) * n + n // 2
            mm = m[:, None]
            upper = (idx >= m)[:, None]
            if fwd:
                blk = np.where(upper, (r >= mm) & (r <= t), (r > t) & (r < mm))
                pair = upper & (idx < m)[None, :]
            else:
                blk = np.where(upper, (r >= mm) & (r < t), (r >= t) & (r < mm))
                pair = (~upper) & (idx >= m)[None, :]
            same = (idx // n)[:, None] == (idx // n)[None, :]
            blocks.append(blk)
            msk.append(pair & same)
        blocks.append(np.ones((SUBLANES, c), bool))
        stack = np.concatenate(blocks, axis=0)
        stacks.append(np.concatenate([stack, stack], axis=1))
        masks.append(np.stack(msk))
    return (np.stack(stacks).astype(np.float32), np.stack(masks).astype(np.float32))


def _gla_kernel(q_ref, k_ref, v_ref, og_ref, r_ref, wa2_ref, ba_ref, ng_ref, wst_ref, msk_ref,
                o_ref, oacc_ref, s_ref, vt_ref, qb_ref, kl_ref, dec_ref, fa_ref, fb_ref, *, chunk):
    t_len, hk = q_ref.shape
    c = chunk
    n_chunks = t_len // c
    n_lvl = msk_ref.shape[1] - 1
    qscale = hk ** -0.5
    nt_dims = (((1,), (1,)), ((), ()))

    def chunk_rows(ci):
        return pl.ds(ci * c if isinstance(ci, int) else pl.multiple_of(ci * c, c), c)

    dirs = (0, 1)

    def gate_logits(ci):
        rr = r_ref[chunk_rows(ci), :].astype(BF16)
        return [jnp.dot(rr, wa2_ref[dr], preferred_element_type=F32) + ba_ref[dr] for dr in dirs]

    def range_sums(z):
        xs = []
        for dr in dirs:
            g = (jnp.minimum(z[dr], 0.0) - jnp.log1p(jnp.exp(-jnp.abs(z[dr])))) * (1.0 / GLA_GATE_TEMP)
            g_hi = g.astype(BF16)
            g_lo = (g - g_hi.astype(F32)).astype(BF16)
            xs.append(jnp.dot(wst_ref[dr], jnp.concatenate([g_hi, g_lo], axis=0),
                              preferred_element_type=F32))
        return xs

    def store_factors(xs, f_ref):
        for dr in dirs:
            f_ref[dr] = jnp.exp(xs[dr])

    def local(ci, f_ref, f_next_ref):
        z_next = gate_logits(jnp.minimum(ci + 1, n_chunks - 1))
        rows = chunk_rows(ci)
        q = q_ref[rows, :] * qscale
        k = k_ref[rows, :]
        v = v_ref[rows, :]
        vb = v.astype(BF16)
        vt_ref[ci] = jnp.transpose(v).astype(BF16)

        def blk(dr, b):
            return f_ref[dr, b * c:(b + 1) * c, :]

        scores = [msk_ref[0, 0] * lax.dot_general(q.astype(BF16), k.astype(BF16), nt_dims,
                                                  preferred_element_type=F32),
                  jnp.zeros((c, c), F32)]
        xs_next = None
        for lvl in range(1, n_lvl + 1):
            for dr in dirs:
                fl = blk(dr, 1 + lvl)
                scores[dr] = scores[dr] + msk_ref[dr, lvl] * lax.dot_general(
                    (q * fl).astype(BF16), (k * fl).astype(BF16), nt_dims,
                    preferred_element_type=F32)
            if lvl == min(GLA_SUMS_AFTER, n_lvl):
                xs_next = range_sums(z_next)
        store_factors(xs_next, f_next_ref)
        oacc_ref[rows, :] = jnp.dot((scores[0] + scores[1]).astype(BF16), vb,
                                    preferred_element_type=F32)
        for dr in dirs:
            qb_ref[dr, rows, :] = (q * blk(dr, 0)).astype(BF16)
            kl_ref[dr, rows, :] = (k * blk(dr, 1)).astype(BF16)
            dec_ref[dr, ci] = f_ref[dr, (2 + n_lvl) * c:, :]

    assert n_chunks % 2 == 0

    def local_pair(p, carry):
        local(2 * p, fa_ref, fb_ref)
        local(2 * p + 1, fb_ref, fa_ref)
        return carry

    store_factors(range_sums(gate_logits(0)), fa_ref)
    lax.fori_loop(0, n_chunks // 2, local_pair, 0)

    s_ref[...] = jnp.zeros_like(s_ref)

    def finish(ci):
        rows = chunk_rows(ci)
        o = oacc_ref[rows, :]
        ms = jnp.mean(o * o, axis=-1, keepdims=True)
        y = o * lax.rsqrt(ms + EPS) * ng_ref[...]
        og = og_ref[rows, :]
        o_ref[rows, :] = (y * (og * _sigmoid(og))).astype(o_ref.dtype)

    def carried(it, carry, *, complete):
        for direction in (0, 1):
            ci = it if direction == 0 else n_chunks - 1 - it
            rows = chunk_rows(ci)
            state = s_ref[direction]
            oacc_ref[rows, :] += lax.dot_general(qb_ref[direction, rows, :], state.astype(BF16),
                                                 nt_dims, preferred_element_type=F32)
            s_ref[direction] = (state * dec_ref[direction, ci][0:1]
                                + jnp.dot(vt_ref[ci], kl_ref[direction, rows, :],
                                          preferred_element_type=F32))
        if complete:
            finish(it)
            finish(n_chunks - 1 - it)
        return carry

    half = n_chunks // 2
    lax.fori_loop(0, half, functools.partial(carried, complete=False), 0, unroll=4)
    lax.fori_loop(half, n_chunks, functools.partial(carried, complete=True), 0, unroll=4)


def _gla_scan(proj, r, wa2p, ba, norm_g, batch, hk, hv):
    n = proj.shape[0]
    t = n // batch
    c = GLA_CHUNK
    heads = GLA_HEADS
    wst, msk = _gla_tables(c)
    wst = jnp.asarray(wst, BF16)
    msk = jnp.asarray(msk, F32)
    kq, kk = 0, heads
    kv, kg = (2 * heads * hk) // hv, (2 * heads * hk) // hv + heads
    return pl.pallas_call(
        functools.partial(_gla_kernel, chunk=c),
        grid=(batch, heads),
        in_specs=[
            pl.BlockSpec((t, hk), lambda b, h: (b, kq + h)),
            pl.BlockSpec((t, hk), lambda b, h: (b, kk + h)),
            pl.BlockSpec((t, hv), lambda b, h: (b, kv + h)),
            pl.BlockSpec((t, hv), lambda b, h: (b, kg + h)),
            pl.BlockSpec((t, r.shape[1]), lambda b, h: (b, 0)),
            pl.BlockSpec((2, wa2p.shape[1], hk), lambda b, h: (0, 0, h)),
            pl.BlockSpec((2, 1, hk), lambda b, h: (0, 0, h)),
            pl.BlockSpec((1, hv), lambda b, h: (0, 0)),
            pl.BlockSpec(wst.shape, lambda b, h: (0, 0, 0)),
            pl.BlockSpec(msk.shape, lambda b, h: (0, 0, 0, 0)),
        ],
        out_specs=pl.BlockSpec((t, hv), lambda b, h: (b, h)),
        out_shape=jax.ShapeDtypeStruct((n, heads * hv), BF16),
        scratch_shapes=[
            pltpu.VMEM((t, hv), F32),
            pltpu.VMEM((2, hv, hk), F32),
            pltpu.VMEM((t // c, hv, c), BF16),
            pltpu.VMEM((2, t, hk), BF16),
            pltpu.VMEM((2, t, hk), BF16),
            pltpu.VMEM((2, t // c, SUBLANES, hk), F32),
            pltpu.VMEM((2, wst.shape[1], hk), F32),
            pltpu.VMEM((2, wst.shape[1], hk), F32),
        ],
        compiler_params=_params("parallel", "arbitrary"),
        name="gla_scan",
    )(proj, proj, proj, proj, r, wa2p, ba, norm_g, wst, msk)


def kernel(x, c, ada_w, ada_b, norm_g, ffn_w_gate, ffn_w_up, ffn_w_down, conv_w_in, conv_b_in,
           conv_w_dw, conv_b_dw, conv_ln_g, conv_ln_b, conv_w_out, conv_b_out, gla_w_in, gla_wa1,
           gla_wa2, gla_ba, gla_norm_g, gla_w_out, pool_w, pool_b, pool_scale, final_g):
    batch, t, d = x.shape
    depth = ada_w.shape[0]
    n = batch * t
    n_mixers = 3

    mods_all = _ada_mods(c, ada_w, ada_b).reshape(depth, batch, N_MOD, 1, d)
    fg = final_g.reshape(1, d)
    conv_w_in_bf, gla_w_in_bf = conv_w_in.astype(BF16), gla_w_in.astype(BF16)

    xs = x.reshape(n, d)
    for i in range(depth):
        mods = mods_all[i]
        ng = norm_g[i].reshape(3, 1, d)
        xs = _ffn(xs, mods, 0, ng[0], ffn_w_gate, ffn_w_up, ffn_w_down, i, 0, fg, t, final=False)

        kind, j = i % n_mixers, i // n_mixers
        if kind == 0:
            u = _norm_mod_matmul(xs, mods, 3, ng[1], conv_w_in_bf, j,
                                 conv_b_in[j].reshape(1, 2 * d), t, glu=True)
            a = _dwconv_ln_silu(u, conv_w_dw[j], conv_b_dw[j].reshape(1, d),
                                conv_ln_g[j].reshape(1, d), conv_ln_b[j].reshape(1, d), batch)
            xs = _proj_residual(a, conv_w_out, j, conv_b_out[j].reshape(1, d),
                                xs, mods, 5, t)
        elif kind == 1:
            dk = gla_wa2.shape[-1]
            rank = gla_wa1.shape[-1]
            hk, hv = dk // GLA_HEADS, d // GLA_HEADS
            wa1p = jnp.zeros((d, LANES), F32)
            wa2p = jnp.zeros((2, LANES, dk), F32)
            for dr in (0, 1):
                wa1p = wa1p.at[:, dr * rank:(dr + 1) * rank].set(gla_wa1[j, dr])
                wa2p = wa2p.at[dr, dr * rank:(dr + 1) * rank].set(gla_wa2[j, dr])
            proj, r = _norm_mod_matmul(xs, mods, 3, ng[1], gla_w_in_bf, j, None, t,
                                       glu=False, wa=wa1p.astype(BF16))
            y = _gla_scan(proj, r, wa2p.astype(BF16), gla_ba[j].reshape(2, 1, dk),
                          gla_norm_g[j].reshape(1, hv), batch, hk, hv)
            xs = _proj_residual(y, gla_w_out, j, None, xs, mods, 5, t)
        else:
            xs = _pool_sublayer(xs, mods, ng[1], pool_w[j].astype(BF16), pool_b[j].reshape(1, d),
                                pool_scale[j].reshape(1, d), batch)

        xs = _ffn(xs, mods, 6, ng[2], ffn_w_gate, ffn_w_up, ffn_w_down, i, 1, fg, t,
                  final=(i == depth - 1))
    return xs.reshape(batch, t, d)
```
